```python
import jax, jax.numpy as jnp
from jax import lax
import numpy as np

D_MODEL = 1024
BATCH = 4
SEQ = 8192
DEPTH = 2
DEC_BATCH = 32
DEC_SEQ = 4
PAST_LEN = 16384
PAGE_SIZE = 128

N_META = 16
HEAD_DIM = 64
D_ATTN = D_MODEL // 2
N_HEADS = D_ATTN // HEAD_DIM
D_CONV = D_MODEL - D_ATTN
D_MIX = D_ATTN + D_CONV
CONV_WIDTH = 31
Q_BLOCK = 128
EPS = 1e-6
SB_BIAS_INIT = -8.0
D_IN = 4 * D_ATTN + 3 * D_CONV
SPLITS = (D_ATTN, 2 * D_ATTN, 3 * D_ATTN, 4 * D_ATTN, 4 * D_ATTN + D_CONV, 4 * D_ATTN + 2 * D_CONV)

kernel_name = "hymba_stickbreak_conformer_decode_step"


def _rmsnorm(x, g):
    xf = x.astype(jnp.float32)
    y = xf * lax.rsqrt(jnp.mean(xf * xf, axis=-1, keepdims=True) + EPS)
    return (y * g.astype(jnp.float32)).astype(x.dtype)


def _layernorm(x, g, b):
    xf = x.astype(jnp.float32)
    xc = xf - jnp.mean(xf, axis=-1, keepdims=True)
    y = xc * lax.rsqrt(jnp.mean(xc * xc, axis=-1, keepdims=True) + EPS)
    return (y * g.astype(jnp.float32) + b.astype(jnp.float32)).astype(x.dtype)


def _stick_breaking(q, k, v, bias, q_pos, k_pos):
    z = jnp.einsum("bqhd,bkhd->bhqk", q.astype(jnp.float32), k.astype(jnp.float32)) * (HEAD_DIM ** -0.5)
    z = z + bias.astype(jnp.float32)[None, :, None, None]
    visible = (k_pos[None, :] < q_pos[:, None])[None, None]
    log_beta = jax.nn.log_sigmoid(z)
    log_keep = jnp.where(visible, jax.nn.log_sigmoid(-z), 0.0)
    suffix = lax.cumsum(log_keep, axis=3, reverse=True)
    log_w = log_beta + (suffix - log_keep)
    w = jnp.where(visible, jnp.exp(log_w), 0.0)
    o = jnp.einsum("bhqk,bkhd->bqhd", w, v.astype(jnp.float32))
    return o.astype(v.dtype)


def _stick_breaking_prompt(q, k, v, bias):
    b, l, h, d = q.shape
    n_real = l - N_META
    nb = n_real // Q_BLOCK
    pos = jnp.arange(l, dtype=jnp.int32)
    o_meta = _stick_breaking(q[:, :N_META], k[:, :N_META], v[:, :N_META], bias, pos[:N_META], pos[:N_META])
    qb = q[:, N_META:].reshape(b, nb, Q_BLOCK, h, d).transpose(1, 0, 2, 3, 4)
    pb = pos[N_META:].reshape(nb, Q_BLOCK)
    ob = lax.map(lambda blk: _stick_breaking(blk[0], k, v, bias, blk[1], pos), (qb, pb))
    o_real = ob.transpose(1, 0, 2, 3, 4).reshape(b, n_real, h, d)
    return jnp.concatenate([o_meta, o_real], axis=1)


def _make_sample_attend(past_k, past_v):
    def attend(q, k, v, bias):
        t = q.shape[1]
        p = past_k.shape[1]
        k_all = jnp.concatenate([past_k.astype(k.dtype), k], axis=1)
        v_all = jnp.concatenate([past_v.astype(v.dtype), v], axis=1)
        k_pos = jnp.arange(p + t, dtype=jnp.int32)
        return _stick_breaking(q, k_all, v_all, bias, k_pos[p:], k_pos)
    return attend


def _causal_dwconv(u, past, w, b):
    full = jnp.concatenate([past.astype(u.dtype), u], axis=1)
    y = lax.conv_general_dilated(full, w[:, None, :].astype(u.dtype), (1,), "VALID",
                                 dimension_numbers=("NWC", "WIO", "NWC"),
                                 feature_group_count=u.shape[-1])
    return y + b.astype(u.dtype), full[:, -(CONV_WIDTH - 1):]


def _layer(x, attend, conv_past, w_in, g_pre, g_post, sb_bias, g_attn_out, conv_w, conv_b,
           ln_g, ln_b, w_pw2, g_conv_out, w_out):
    bsz, t, _ = x.shape
    hn = _rmsnorm(x, g_pre)
    proj = hn @ w_in
    q, k, v, gate_a, glu_val, glu_gate, gate_c = jnp.split(proj, SPLITS, axis=-1)
    q = q.reshape(bsz, t, N_HEADS, HEAD_DIM)
    k = k.reshape(bsz, t, N_HEADS, HEAD_DIM)
    v = v.reshape(bsz, t, N_HEADS, HEAD_DIM)
    o_a = attend(q, k, v, sb_bias).reshape(bsz, t, D_ATTN)
    o_a = _rmsnorm(o_a, g_attn_out) * jax.nn.silu(gate_a)
    u = glu_val * jax.nn.sigmoid(glu_gate)
    c, conv_tail = _causal_dwconv(u, conv_past, conv_w, conv_b)
    c = jax.nn.silu(_layernorm(c, ln_g, ln_b)) @ w_pw2
    o_c = _rmsnorm(c, g_conv_out) * jax.nn.silu(gate_c)
    y = jnp.concatenate([o_a, o_c], axis=-1) @ w_out
    return x + _rmsnorm(y, g_post), k, v, conv_tail


def setup_inputs(seed: int = 0) -> dict:
    key = jax.random.key(seed)
    ks = jax.random.split(key, 20)
    n_pages = PAST_LEN // PAGE_SIZE
    n_used = DEC_BATCH * n_pages
    n_pool = (5 * n_used + 3) // 4
    f32 = jnp.float32

    def nrm(k, shape, scale):
        return jax.random.normal(k, shape, f32) * scale

    x_prompt = nrm(ks[0], (BATCH, SEQ, D_MODEL), 1.0)
    x_sample = nrm(ks[1], (DEC_BATCH, DEC_SEQ, D_MODEL), 1.0)
    cache_k = nrm(ks[2], (DEPTH, n_pool, PAGE_SIZE, N_HEADS, HEAD_DIM), 1.0)
    cache_v = nrm(ks[3], (DEPTH, n_pool, PAGE_SIZE, N_HEADS, HEAD_DIM), 1.0)
    state_conv = nrm(ks[4], (DEPTH, DEC_BATCH, CONV_WIDTH - 1, D_CONV), 0.5)
    page_table = jax.random.permutation(ks[5], n_pool)[:n_used].reshape(DEC_BATCH, n_pages).astype(jnp.int32)
    meta_tokens = nrm(ks[6], (N_META, D_MODEL), 1.0)
    w_in = nrm(ks[7], (DEPTH, D_MODEL, D_IN), D_MODEL ** -0.5)
    g_pre = 1.0 + nrm(ks[8], (DEPTH, D_MODEL), 0.01)
    g_post = 1.0 + nrm(ks[9], (DEPTH, D_MODEL), 0.01)
    sb_bias = SB_BIAS_INIT + nrm(ks[18], (DEPTH, N_HEADS), 0.1)
    g_attn_out = 1.0 + nrm(ks[10], (DEPTH, D_ATTN), 0.01)
    conv_w = nrm(ks[11], (DEPTH, CONV_WIDTH, D_CONV), CONV_WIDTH ** -0.5)
    conv_b = nrm(ks[12], (DEPTH, D_CONV), 0.01)
    ln_g = 1.0 + nrm(ks[13], (DEPTH, D_CONV), 0.01)
    ln_b = nrm(ks[14], (DEPTH, D_CONV), 0.01)
    w_pw2 = nrm(ks[15], (DEPTH, D_CONV, D_CONV), D_CONV ** -0.5)
    g_conv_out = 1.0 + nrm(ks[16], (DEPTH, D_CONV), 0.01)
    w_out = nrm(ks[17], (DEPTH, D_MIX, D_MODEL), D_MIX ** -0.5)
    return {"x_prompt": x_prompt, "x_sample": x_sample, "cache_k": cache_k, "cache_v": cache_v,
            "state_conv": state_conv, "page_table": page_table, "meta_tokens": meta_tokens,
            "w_in": w_in, "g_pre": g_pre, "g_post": g_post, "sb_bias": sb_bias, "g_attn_out": g_attn_out,
            "conv_w": conv_w, "conv_b": conv_b, "ln_g": ln_g, "ln_b": ln_b,
            "w_pw2": w_pw2, "g_conv_out": g_conv_out, "w_out": w_out}


def reference(x_prompt, x_sample, cache_k, cache_v, state_conv, page_table, meta_tokens,
              w_in, g_pre, g_post, sb_bias, g_attn_out, conv_w, conv_b, ln_g, ln_b, w_pw2, g_conv_out, w_out):
    n_pages = PAST_LEN // PAGE_SIZE
    bp = x_prompt.shape[0]
    bs = x_sample.shape[0]
    meta = jnp.broadcast_to(meta_tokens.astype(x_prompt.dtype)[None], (bp, N_META, D_MODEL))
    h_p = jnp.concatenate([meta, x_prompt], axis=1)
    h_s = x_sample
    conv_zero = jnp.zeros((bp, CONV_WIDTH - 1, D_CONV), x_prompt.dtype)
    kp_l, vp_l, cp_l, ks_l, vs_l, cs_l = [], [], [], [], [], []
    for l in range(DEPTH):
        lw = (w_in[l], g_pre[l], g_post[l], sb_bias[l], g_attn_out[l], conv_w[l], conv_b[l],
              ln_g[l], ln_b[l], w_pw2[l], g_conv_out[l], w_out[l])
        h_p, kp, vp, cp = _layer(h_p, _stick_breaking_prompt, conv_zero, *lw)
        past_k = cache_k[l][page_table].reshape(bs, n_pages * PAGE_SIZE, N_HEADS, HEAD_DIM)
        past_v = cache_v[l][page_table].reshape(bs, n_pages * PAGE_SIZE, N_HEADS, HEAD_DIM)
        h_s, ks_, vs_, cs_ = _layer(h_s, _make_sample_attend(past_k, past_v), state_conv[l], *lw)
        kp_l.append(kp); vp_l.append(vp); cp_l.append(cp)
        ks_l.append(ks_); vs_l.append(vs_); cs_l.append(cs_)
    y_prompt = h_p[:, N_META:]
    return (y_prompt, h_s, jnp.stack(kp_l), jnp.stack(vp_l), jnp.stack(cp_l),
            jnp.stack(ks_l), jnp.stack(vs_l), jnp.stack(cs_l))
```

```python
import functools
import math

import jax
import jax.numpy as jnp
from jax import lax
from jax.experimental import pallas as pl
from jax.experimental.pallas import tpu as pltpu

N_META = 16
HEAD_DIM = 64
CONV_WIDTH = 31
EPS = 1e-6
LOG2E = 1.4426950408889634

LANES = 128
HEADS_PER_BLOCK = LANES // HEAD_DIM
ATT_TILE = 256
ROW_TILE = 256
HALO = 32
CONV_CHUNK = 64
PAGES_PER_STEP = 8
VMEM_LIMIT = 48 * 1024 * 1024

F32 = jnp.float32
BF16 = jnp.bfloat16
NT_DIMS = (((1,), (1,)), ((), ()))


def _rms_scale(x):
    return lax.rsqrt(jnp.mean(x * x, axis=-1, keepdims=True) + EPS)


def _silu(x):
    return x * jax.nn.sigmoid(x)


def _softplus2(t):
    return jnp.maximum(t, 0.0) + jnp.log2(1.0 + jnp.exp2(-jnp.abs(t)))


def _project(x, win_ref, gpre_ref, d_attn, d_conv):
    hn = (x * _rms_scale(x) * gpre_ref[...]).astype(BF16)
    cols = [0, d_attn, 2 * d_attn, 3 * d_attn, 4 * d_attn, 4 * d_attn + d_conv, 4 * d_attn + 2 * d_conv,
            4 * d_attn + 3 * d_conv]
    return [jnp.dot(hn, win_ref[:, cols[i]:cols[i + 1]], preferred_element_type=F32) for i in range(7)]


def _conv_tail(c, gate_c, lng_ref, lnb_ref, wpw2_ref, gco_ref):
    mu = jnp.mean(c, axis=-1, keepdims=True)
    cc = c - mu
    y = cc * lax.rsqrt(jnp.mean(cc * cc, axis=-1, keepdims=True) + EPS) * lng_ref[...] + lnb_ref[...]
    p = jnp.dot(_silu(y).astype(BF16), wpw2_ref[...], preferred_element_type=F32)
    return p * _rms_scale(p) * gco_ref[...] * _silu(gate_c)


def _proj_prompt_kernel(x_ref, win_ref, gpre_ref, convw_ref, convb_ref, lng_ref, lnb_ref, wpw2_ref, gco_ref,
                        q_ref, kb_ref, vb_ref, kf_ref, vf_ref, ga_ref, mc_ref, tail_ref,
                        ubuf, ush, cbuf, *, tm, d_attn, d_conv, tail_block, tail_off, qscale):
    i = pl.program_id(1)

    @pl.when(i == 0)
    def _():
        ubuf[0:HALO, :] = jnp.zeros((HALO, d_conv), F32)

    q, k, v, gate_a, glu_val, glu_gate, gate_c = _project(x_ref[0], win_ref, gpre_ref, d_attn, d_conv)
    q_ref[0] = (q * qscale).astype(BF16)
    kf_ref[0] = k
    kb_ref[0] = k.astype(BF16)
    vf_ref[0] = v
    vb_ref[0] = v.astype(BF16)
    ga_ref[0] = _silu(gate_a).astype(BF16)

    ubuf[HALO:HALO + tm, :] = glu_val * jax.nn.sigmoid(glu_gate)
    span = tm + HALO - 8
    for r in range(1, 8):
        ush[r, 0:span, :] = ubuf[r:r + span, :]

    def chunk(ci, carry):
        r0 = pl.multiple_of(ci * CONV_CHUNK, CONV_CHUNK)
        acc = jnp.broadcast_to(convb_ref[...], (CONV_CHUNK, d_conv))
        for j in range(CONV_WIDTH):
            off = HALO - (CONV_WIDTH - 1) + j
            a8, r = (off // 8) * 8, off % 8
            if r == 0:
                src = ubuf[pl.ds(r0 + a8, CONV_CHUNK), :]
            else:
                src = ush[r, pl.ds(r0 + a8, CONV_CHUNK), :]
            acc = acc + convw_ref[j:j + 1, :] * src
        cbuf[pl.ds(r0, CONV_CHUNK), :] = acc
        return carry

    lax.fori_loop(0, tm // CONV_CHUNK, chunk, 0)
    mc_ref[0] = _conv_tail(cbuf[...], gate_c, lng_ref, lnb_ref, wpw2_ref, gco_ref).astype(BF16)

    @pl.when(i == tail_block)
    def _():
        tail_ref[0] = ubuf[tail_off:tail_off + HALO, :]

    ubuf[0:HALO, :] = ubuf[tm:tm + HALO, :]


def _proj_sample_kernel(x_ref, state_ref, win_ref, gpre_ref, convw_ref, convb_ref, lng_ref, lnb_ref, wpw2_ref,
                        gco_ref, q_ref, kf_ref, vf_ref, ga_ref, mc_ref, u_ref, *, nb, nt, d_attn, d_conv, qscale):
    q, k, v, gate_a, glu_val, glu_gate, gate_c = _project(x_ref[...], win_ref, gpre_ref, d_attn, d_conv)
    q_ref[...] = q * qscale
    kf_ref[...] = k
    vf_ref[...] = v
    ga_ref[...] = _silu(gate_a).astype(BF16)
    u = glu_val * jax.nn.sigmoid(glu_gate)
    u_ref[...] = u
    hist = CONV_WIDTH - 1
    cs = []
    for t in range(nt):
        acc = jnp.broadcast_to(convb_ref[...], (nb, d_conv))
        for j in range(CONV_WIDTH):
            p = t + j
            src = state_ref[p] if p < hist else u[(p - hist) * nb:(p - hist + 1) * nb, :]
            acc = acc + convw_ref[j:j + 1, :] * src
        cs.append(acc)
    c = jnp.concatenate(cs, axis=0)
    mc_ref[...] = _conv_tail(c, gate_c, lng_ref, lnb_ref, wpw2_ref, gco_ref).astype(BF16)


def _out_kernel(oa_ref, ga_ref, mc_ref, x_ref, wout_ref, gattn_ref, gpost_ref, y_ref, *, d_attn):
    oa = oa_ref[0]
    ma = (oa * _rms_scale(oa) * gattn_ref[...] * ga_ref[0].astype(F32)).astype(BF16)
    y = jnp.dot(ma, wout_ref[0:d_attn, :], preferred_element_type=F32)
    y = y + jnp.dot(mc_ref[0], wout_ref[d_attn:, :], preferred_element_type=F32)
    y_ref[0] = x_ref[0] + y * _rms_scale(y) * gpost_ref[...]


def _attn_prompt_kernel(bias_ref, q_ref, k_ref, v_ref, tri_ref, o_ref, acc_ref, carry_ref):
    hp = pl.program_id(1)
    qi = pl.program_id(2)
    tq = ATT_TILE
    q = q_ref[0]
    lane = lax.broadcasted_iota(jnp.int32, (tq, LANES), 1)
    zero = jnp.zeros_like(q)
    qm = [jnp.where(lane < HEAD_DIM, q, zero), jnp.where(lane >= HEAD_DIM, q, zero)]
    bias = [bias_ref[HEADS_PER_BLOCK * hp], bias_ref[HEADS_PER_BLOCK * hp + 1]]

    acc_ref[...] = jnp.zeros_like(acc_ref)
    carry_ref[...] = jnp.zeros_like(carry_ref)

    def tile(kb, diag):
        start = pl.multiple_of(kb * tq, tq)
        kblk = k_ref[0, pl.ds(start, tq), :]
        vblk = v_ref[0, pl.ds(start, tq), :]
        if diag:
            row = lax.broadcasted_iota(jnp.int32, (tq, tq), 0)
            col = lax.broadcasted_iota(jnp.int32, (tq, tq), 1)
            vis = col < row
        for h in range(HEADS_PER_BLOCK):
            t = lax.dot_general(qm[h], kblk, NT_DIMS, preferred_element_type=F32) + bias[h]
            sp = _softplus2(t)
            spm = jnp.where(vis, sp, 0.0) if diag else sp
            s_in = jnp.dot(spm.astype(BF16), tri_ref[...], preferred_element_type=F32)
            w = jnp.exp2((t - sp) - s_in - carry_ref[h])
            if diag:
                w = jnp.where(vis, w, 0.0)
            acc_ref[h] += jnp.dot(w.astype(BF16), vblk, preferred_element_type=F32)
            carry_ref[h] += s_in[:, 0:1] + spm[:, 0:1]

    tile(qi, True)

    def body(i, c):
        tile(qi - 1 - i, False)
        return c

    lax.fori_loop(0, qi, body, 0)
    o_ref[0] = jnp.where(lane < HEAD_DIM, acc_ref[0], acc_ref[1])


def _attn_sample_kernel(pt_ref, q_ref, kn_ref, vn_ref, bias_ref, tri_ref, *rest, nt, n_heads, g_pages):
    del pt_ref
    kp_refs = rest[0:g_pages]
    vp_refs = rest[g_pages:2 * g_pages]
    o_ref, qbd_ref, acc_ref, carry_ref = rest[2 * g_pages:]
    s = pl.program_id(1)
    rows = nt * n_heads
    d_attn = n_heads * HEAD_DIM
    page = kn_ref.shape[1]

    head_row = lax.broadcasted_iota(jnp.int32, (n_heads, d_attn), 0)
    head_lane = lax.broadcasted_iota(jnp.int32, (n_heads, d_attn), 1) // HEAD_DIM
    own = head_row == head_lane

    def block(kblk, vblk, vis):
        t = lax.dot_general(qbd_ref[...], kblk, NT_DIMS, preferred_element_type=F32) + bias_ref[...]
        sp = _softplus2(t)
        spm = sp if vis is None else jnp.where(vis, sp, 0.0)
        s_in = jnp.dot(spm.astype(BF16), tri_ref[...], preferred_element_type=F32)
        w = jnp.exp2((t - sp) - s_in - carry_ref[...])
        if vis is not None:
            w = jnp.where(vis, w, 0.0)
        acc_ref[...] += jnp.dot(w.astype(BF16), vblk, preferred_element_type=F32)
        carry_ref[...] += s_in[:, 0:1] + spm[:, 0:1]

    @pl.when(s == 0)
    def _():
        q = q_ref[0]
        parts = [jnp.where(own, jnp.broadcast_to(q[t:t + 1, :], (n_heads, d_attn)), 0.0) for t in range(nt)]
        qbd_ref[...] = jnp.concatenate(parts, axis=0).astype(BF16)
        acc_ref[...] = jnp.zeros_like(acc_ref)
        carry_ref[...] = jnp.zeros_like(carry_ref)
        r_step = lax.broadcasted_iota(jnp.int32, (rows, page), 0) // n_heads
        col = lax.broadcasted_iota(jnp.int32, (rows, page), 1)
        block(kn_ref[0], vn_ref[0], col < r_step)

    def heads_on_lanes(ref):
        return jnp.concatenate([ref[:, h, :] for h in range(n_heads)], axis=-1).astype(BF16)

    for g in range(g_pages):
        block(heads_on_lanes(kp_refs[g]), heads_on_lanes(vp_refs[g]), None)

    @pl.when(s == pl.num_programs(1) - 1)
    def _():
        acc = acc_ref[...]
        outs = [jnp.sum(jnp.where(own, acc[t * n_heads:(t + 1) * n_heads, :], 0.0), axis=0, keepdims=True)
                for t in range(nt)]
        o_ref[0] = jnp.concatenate(outs, axis=0)


def _full(shape):
    n = len(shape)
    return pl.BlockSpec(shape, lambda *_: (0,) * n)


def _tri(n):
    j = lax.broadcasted_iota(jnp.int32, (n, n), 0)
    s = lax.broadcasted_iota(jnp.int32, (n, n), 1)
    return (j > s).astype(BF16)


def _row(a):
    return a.reshape(1, -1).astype(F32)


def _proj_prompt(x, lw, n_real, qscale):
    bsz, lp, d_model = x.shape
    d_attn, d_conv = lw["d_attn"], lw["d_conv"]
    tm = ROW_TILE
    nblk = lp // tm
    tail_block = (n_real - 1) // tm
    tail_off = n_real - tail_block * tm
    assert tail_off % 8 == 0 and tail_off >= 8
    kern = functools.partial(_proj_prompt_kernel, tm=tm, d_attn=d_attn, d_conv=d_conv, tail_block=tail_block,
                             tail_off=tail_off, qscale=qscale)
    blk = lambda w: pl.BlockSpec((1, tm, w), lambda b, i: (b, i, 0))
    bf = lambda w: jax.ShapeDtypeStruct((bsz, lp, w), BF16)
    return pl.pallas_call(
        kern,
        grid=(bsz, nblk),
        in_specs=[blk(d_model), _full(lw["w_in"].shape), _full((1, d_model)), _full((CONV_WIDTH, d_conv)),
                  _full((1, d_conv)), _full((1, d_conv)), _full((1, d_conv)), _full((d_conv, d_conv)),
                  _full((1, d_conv))],
        out_specs=[blk(d_attn), blk(d_attn), blk(d_attn), blk(d_attn), blk(d_attn), blk(d_attn), blk(d_conv),
                   pl.BlockSpec((1, HALO, d_conv), lambda b, i: (b, 0, 0))],
        out_shape=[bf(d_attn), bf(d_attn), bf(d_attn),
                   jax.ShapeDtypeStruct((bsz, n_real, d_attn), F32), jax.ShapeDtypeStruct((bsz, n_real, d_attn), F32),
                   bf(d_attn), bf(d_conv), jax.ShapeDtypeStruct((bsz, HALO, d_conv), F32)],
        scratch_shapes=[pltpu.VMEM((tm + HALO, d_conv), F32), pltpu.VMEM((8, tm + HALO, d_conv), F32),
                        pltpu.VMEM((tm, d_conv), F32)],
        compiler_params=pltpu.CompilerParams(dimension_semantics=("arbitrary", "arbitrary"),
                                             vmem_limit_bytes=VMEM_LIMIT),
        name="proj_prompt",
    )(x, lw["w_in"], lw["g_pre"], lw["conv_w"], lw["conv_b"], lw["ln_g"], lw["ln_b"], lw["w_pw2"], lw["g_conv_out"])


def _proj_sample(x, state_t, lw, nb, nt, qscale):
    rows, d_model = x.shape
    d_attn, d_conv = lw["d_attn"], lw["d_conv"]
    kern = functools.partial(_proj_sample_kernel, nb=nb, nt=nt, d_attn=d_attn, d_conv=d_conv, qscale=qscale)
    f = lambda w, dt: jax.ShapeDtypeStruct((rows, w), dt)
    return pl.pallas_call(
        kern,
        out_shape=[f(d_attn, F32), f(d_attn, F32), f(d_attn, F32), f(d_attn, BF16), f(d_conv, BF16), f(d_conv, F32)],
        compiler_params=pltpu.CompilerParams(vmem_limit_bytes=VMEM_LIMIT),
        name="proj_sample",
    )(x, state_t, lw["w_in"], lw["g_pre"], lw["conv_w"], lw["conv_b"], lw["ln_g"], lw["ln_b"], lw["w_pw2"],
      lw["g_conv_out"])


def _out_proj(oa, ga, mc, x, lw, tm):
    bsz, lp, d_model = x.shape
    d_attn, d_conv = lw["d_attn"], lw["d_conv"]
    blk = lambda w: pl.BlockSpec((1, tm, w), lambda b, i: (b, i, 0))
    return pl.pallas_call(
        functools.partial(_out_kernel, d_attn=d_attn),
        grid=(bsz, lp // tm),
        in_specs=[blk(d_attn), blk(d_attn), blk(d_conv), blk(d_model), _full(lw["w_out"].shape),
                  _full((1, d_attn)), _full((1, d_model))],
        out_specs=blk(d_model),
        out_shape=jax.ShapeDtypeStruct((bsz, lp, d_model), F32),
        compiler_params=pltpu.CompilerParams(dimension_semantics=("arbitrary", "arbitrary"),
                                             vmem_limit_bytes=VMEM_LIMIT),
        name="out_proj",
    )(oa, ga, mc, x, lw["w_out"], lw["g_attn_out"], lw["g_post"])


def _attn_prompt(q, kb, vb, bias2, tri):
    bsz, lp, d_attn = q.shape
    tq = ATT_TILE
    n_pairs = d_attn // LANES
    qblk = pl.BlockSpec((1, tq, LANES), lambda b, hp, qi: (b, qi, hp))
    kvblk = pl.BlockSpec((1, lp, LANES), lambda b, hp, qi: (b, 0, hp))
    return pl.pallas_call(
        _attn_prompt_kernel,
        grid=(bsz, n_pairs, lp // tq),
        in_specs=[pl.BlockSpec(memory_space=pltpu.SMEM), qblk, kvblk, kvblk, _full((tq, tq))],
        out_specs=qblk,
        out_shape=jax.ShapeDtypeStruct((bsz, lp, d_attn), F32),
        scratch_shapes=[pltpu.VMEM((HEADS_PER_BLOCK, tq, LANES), F32), pltpu.VMEM((HEADS_PER_BLOCK, tq, 1), F32)],
        compiler_params=pltpu.CompilerParams(dimension_semantics=("arbitrary", "arbitrary", "arbitrary"),
                                             vmem_limit_bytes=VMEM_LIMIT),
        name="attn_prompt",
    )(bias2, q, kb, vb, tri)


def _attn_sample(q, k_new, v_new, bias_rows, tri, cache_k, cache_v, layer, pt_flat, n_pages, n_heads):
    nb, nt, d_attn = q.shape
    page = cache_k.shape[2]
    g_pages = math.gcd(PAGES_PER_STEP, n_pages)
    n_steps = n_pages // g_pages
    rows = nt * n_heads

    def page_spec(g):
        return pl.BlockSpec((None, None, page, n_heads, HEAD_DIM),
                            lambda b, s, pt: (layer, pt[b * n_pages + n_pages - 1 - (s * g_pages + g)], 0, 0, 0))

    per_seq = lambda shape: pl.BlockSpec(shape, lambda b, s, pt: (b, 0, 0))
    const = lambda shape: pl.BlockSpec(shape, lambda b, s, pt: (0, 0))
    grid_spec = pltpu.PrefetchScalarGridSpec(
        num_scalar_prefetch=1,
        grid=(nb, n_steps),
        in_specs=[per_seq((1, nt, d_attn)), per_seq((1, page, d_attn)), per_seq((1, page, d_attn)),
                  const((rows, 1)), const((page, page))]
                 + [page_spec(g) for g in range(g_pages)] + [page_spec(g) for g in range(g_pages)],
        out_specs=per_seq((1, nt, d_attn)),
        scratch_shapes=[pltpu.VMEM((rows, d_attn), BF16), pltpu.VMEM((rows, d_attn), F32),
                        pltpu.VMEM((rows, 1), F32)],
    )
    kern = functools.partial(_attn_sample_kernel, nt=nt, n_heads=n_heads, g_pages=g_pages)
    return pl.pallas_call(
        kern,
        grid_spec=grid_spec,
        out_shape=jax.ShapeDtypeStruct((nb, nt, d_attn), F32),
        compiler_params=pltpu.CompilerParams(dimension_semantics=("arbitrary", "arbitrary"),
                                             vmem_limit_bytes=VMEM_LIMIT),
        name="attn_sample",
    )(pt_flat, q, k_new, v_new, bias_rows, tri, *([cache_k] * g_pages), *([cache_v] * g_pages))


def kernel(x_prompt, x_sample, cache_k, cache_v, state_conv, page_table, meta_tokens,
           w_in, g_pre, g_post, sb_bias, g_attn_out, conv_w, conv_b, ln_g, ln_b, w_pw2, g_conv_out, w_out):
    depth = w_in.shape[0]
    bp, seq, d_model = x_prompt.shape
    nb, nt, _ = x_sample.shape
    n_pool, page, n_heads, head_dim = cache_k.shape[1:]
    assert head_dim == HEAD_DIM
    d_attn = n_heads * head_dim
    d_conv = w_pw2.shape[-1]
    n_pages = page_table.shape[1]
    n_real = N_META + seq
    lp = -(-n_real // ATT_TILE) * ATT_TILE
    qscale = LOG2E * HEAD_DIM ** -0.5

    meta = jnp.broadcast_to(meta_tokens.astype(x_prompt.dtype)[None], (bp, N_META, d_model))
    h_p = jnp.concatenate([meta, x_prompt, jnp.zeros((bp, lp - n_real, d_model), x_prompt.dtype)], axis=1)
    h_s = x_sample.transpose(1, 0, 2).reshape(nt * nb, d_model)
    pt_flat = page_table.reshape(-1).astype(jnp.int32)
    tri_p = _tri(ATT_TILE)
    tri_s = _tri(page)

    kp_l, vp_l, cp_l, ks_l, vs_l, cs_l = [], [], [], [], [], []
    for l in range(depth):
        lw = dict(d_attn=d_attn, d_conv=d_conv, w_in=w_in[l].astype(BF16), g_pre=_row(g_pre[l]),
                  g_post=_row(g_post[l]), g_attn_out=_row(g_attn_out[l]), conv_w=conv_w[l].astype(F32),
                  conv_b=_row(conv_b[l]), ln_g=_row(ln_g[l]), ln_b=_row(ln_b[l]), w_pw2=w_pw2[l].astype(BF16),
                  g_conv_out=_row(g_conv_out[l]), w_out=w_out[l].astype(BF16))
        bias2 = sb_bias[l].astype(F32) * LOG2E

        q, kb, vb, kf, vf, ga, mc, tail = _proj_prompt(h_p, lw, n_real, qscale)
        oa = _attn_prompt(q, kb, vb, bias2, tri_p)
        h_p = _out_proj(oa, ga, mc, h_p, lw, ROW_TILE)
        kp_l.append(kf.reshape(bp, n_real, n_heads, head_dim))
        vp_l.append(vf.reshape(bp, n_real, n_heads, head_dim))
        cp_l.append(tail[:, HALO - (CONV_WIDTH - 1):])

        state_t = state_conv[l].astype(F32).transpose(1, 0, 2)
        qs, ksf, vsf, gas, mcs, us = _proj_sample(h_s, state_t, lw, nb, nt, qscale)
        to_seq = lambda a: a.reshape(nt, nb, -1).transpose(1, 0, 2)
        pad_page = lambda a: jnp.pad(to_seq(a).astype(BF16), ((0, 0), (0, page - nt), (0, 0)))
        bias_rows = jnp.tile(bias2, nt).reshape(nt * n_heads, 1)
        oas = _attn_sample(to_seq(qs), pad_page(ksf), pad_page(vsf), bias_rows, tri_s, cache_k, cache_v, l, pt_flat,
                           n_pages, n_heads)
        oas = oas.transpose(1, 0, 2).reshape(1, nt * nb, d_attn)
        h_s = _out_proj(oas, gas[None], mcs[None], h_s[None], lw, nt * nb)[0]
        ks_l.append(to_seq(ksf).reshape(nb, nt, n_heads, head_dim))
        vs_l.append(to_seq(vsf).reshape(nb, nt, n_heads, head_dim))
        cs_l.append(jnp.concatenate([state_conv[l][:, nt:], to_seq(us).astype(state_conv.dtype)], axis=1))

    y_prompt = h_p[:, N_META:n_real]
    y_sample = h_s.reshape(nt, nb, d_model).transpose(1, 0, 2)
    return (y_prompt, y_sample, jnp.stack(kp_l), jnp.stack(vp_l), jnp.stack(cp_l),
            jnp.stack(ks_l), jnp.stack(vs_l), jnp.stack(cs_l))
```

```python
import functools
import math

import jax
import jax.numpy as jnp
from jax import lax
from jax.experimental import pallas as pl
from jax.experimental.pallas import tpu as pltpu

N_META = 16
HEAD_DIM = 64
CONV_WIDTH = 31
EPS = 1e-6
LOG2E = 1.4426950408889634

LANES = 128
HEADS_PER_BLOCK = LANES // HEAD_DIM
ATT_TILE = 256
KEY_UNROLL = 2
ROW_TILE = ATT_TILE
HALO = 32
CONV_CHUNK = 64
PAGES_PER_STEP = 16
VMEM_LIMIT = 56 * 1024 * 1024

F32 = jnp.float32
BF16 = jnp.bfloat16
NT_DIMS = (((1,), (1,)), ((), ()))


def _rms_scale(x):
    return lax.rsqrt(jnp.mean(x * x, axis=-1, keepdims=True) + EPS)


def _silu(x):
    return x * jax.nn.sigmoid(x)


def _softplus2(t):
    return jnp.maximum(t, 0.0) + jnp.log2(1.0 + jnp.exp2(-jnp.abs(t)))


def _normed(x, gpre_ref):
    return (x * _rms_scale(x) * gpre_ref[...]).astype(BF16)


def _conv_tail(c, gate_c, lng_ref, lnb_ref, wpw2_ref, gco_ref):
    mu = jnp.mean(c, axis=-1, keepdims=True)
    cc = c - mu
    y = cc * lax.rsqrt(jnp.mean(cc * cc, axis=-1, keepdims=True) + EPS) * lng_ref[...] + lnb_ref[...]
    p = jnp.dot(_silu(y).astype(BF16), wpw2_ref[...], preferred_element_type=F32)
    return p * _rms_scale(p) * gco_ref[...] * _silu(gate_c)


def _proj_prompt_kernel(x_ref, wqkvt_ref, wk_ref, wrest_ref, gpre_ref, convw_ref, convb_ref, lng_ref, lnb_ref,
                        wpw2_ref, gco_ref,
                        qt_ref, kb_ref, vt_ref, kft_ref, vft_ref, ga_ref, mc_ref, tail_ref,
                        ubuf, ush, cbuf, *, tm, d_attn, d_conv, tail_block, tail_off, qscale):
    i = pl.program_id(1)

    @pl.when(i == 0)
    def _():
        ubuf[0:HALO, :] = jnp.zeros((HALO, d_conv), F32)

    hn = _normed(x_ref[0], gpre_ref)
    qkv_t = lax.dot_general(wqkvt_ref[...], hn, NT_DIMS, preferred_element_type=F32)
    qt_ref[0] = (qkv_t[0:d_attn] * qscale).astype(BF16)
    k_t = qkv_t[d_attn:2 * d_attn]
    v_t = qkv_t[2 * d_attn:3 * d_attn]
    kft_ref[0] = k_t
    vft_ref[0] = v_t
    vt_ref[0, 0] = v_t.astype(BF16)
    kb_ref[0] = jnp.dot(hn, wk_ref[...], preferred_element_type=F32).astype(BF16)

    rest = lambda c: jnp.dot(hn, wrest_ref[:, c * d_conv:(c + 1) * d_conv], preferred_element_type=F32)
    ga_ref[0] = _silu(rest(0)).astype(BF16)

    ubuf[HALO:HALO + tm, :] = rest(1) * jax.nn.sigmoid(rest(2))
    span = tm + HALO - 8
    for r in range(1, 8):
        ush[r, 0:span, :] = ubuf[r:r + span, :]

    def chunk(ci, carry):
        r0 = pl.multiple_of(ci * CONV_CHUNK, CONV_CHUNK)
        acc = jnp.broadcast_to(convb_ref[...], (CONV_CHUNK, d_conv))
        for r in range(8):
            taps = [(j, (HALO - (CONV_WIDTH - 1) + j) // 8) for j in range(CONV_WIDTH)
                    if (HALO - (CONV_WIDTH - 1) + j) % 8 == r]
            a_lo = min(a for _, a in taps)
            a_hi = max(a for _, a in taps)
            rows = CONV_CHUNK + 8 * (a_hi - a_lo)
            if r == 0:
                win = ubuf[pl.ds(r0 + 8 * a_lo, rows), :]
            else:
                win = ush[r, pl.ds(r0 + 8 * a_lo, rows), :]
            for j, a in taps:
                acc = acc + convw_ref[j:j + 1, :] * win[8 * (a - a_lo):8 * (a - a_lo) + CONV_CHUNK, :]
        cbuf[pl.ds(r0, CONV_CHUNK), :] = acc
        return carry

    lax.fori_loop(0, tm // CONV_CHUNK, chunk, 0)
    mc_ref[0] = _conv_tail(cbuf[...], rest(3), lng_ref, lnb_ref, wpw2_ref, gco_ref).astype(BF16)

    @pl.when(i == tail_block)
    def _():
        tail_ref[0] = ubuf[tail_off:tail_off + HALO, :]

    ubuf[0:HALO, :] = ubuf[tm:tm + HALO, :]


def _proj_sample_kernel(x_ref, state_ref, win_ref, gpre_ref, convw_ref, convb_ref, lng_ref, lnb_ref, wpw2_ref,
                        gco_ref, q_ref, kf_ref, vf_ref, ga_ref, mc_ref, u_ref, *, nb, nt, d_attn, d_conv, qscale):
    hn = _normed(x_ref[...], gpre_ref)
    cols = [0, d_attn, 2 * d_attn, 3 * d_attn, 4 * d_attn, 4 * d_attn + d_conv, 4 * d_attn + 2 * d_conv,
            4 * d_attn + 3 * d_conv]
    q, k, v, gate_a, glu_val, glu_gate, gate_c = [
        jnp.dot(hn, win_ref[:, cols[i]:cols[i + 1]], preferred_element_type=F32) for i in range(7)]
    q_ref[...] = q * qscale
    kf_ref[...] = k
    vf_ref[...] = v
    ga_ref[...] = _silu(gate_a).astype(BF16)
    u = glu_val * jax.nn.sigmoid(glu_gate)
    u_ref[...] = u
    hist = CONV_WIDTH - 1
    cs = []
    for t in range(nt):
        acc = jnp.broadcast_to(convb_ref[...], (nb, d_conv))
        for j in range(CONV_WIDTH):
            p = t + j
            src = state_ref[p] if p < hist else u[(p - hist) * nb:(p - hist + 1) * nb, :]
            acc = acc + convw_ref[j:j + 1, :] * src
        cs.append(acc)
    c = jnp.concatenate(cs, axis=0)
    mc_ref[...] = _conv_tail(c, gate_c, lng_ref, lnb_ref, wpw2_ref, gco_ref).astype(BF16)


def _out_kernel(oa_ref, ga_ref, mc_ref, x_ref, wout_ref, gattn_ref, gpost_ref, y_ref, *, d_attn):
    oa = oa_ref[0]
    ma = (oa * _rms_scale(oa) * gattn_ref[...] * ga_ref[0].astype(F32)).astype(BF16)
    y = jnp.dot(ma, wout_ref[0:d_attn, :], preferred_element_type=F32)
    y = y + jnp.dot(mc_ref[0], wout_ref[d_attn:, :], preferred_element_type=F32)
    y_ref[0] = x_ref[0] + y * _rms_scale(y) * gpost_ref[...]


def _attn_prompt_kernel(qt_ref, brows_ref, k_ref, ones_ref, vt_ref, tri_ref, o_ref, acc_ref, carry_ref):
    qi = pl.program_id(2)
    tq = tk = ATT_TILE
    qt = qt_ref[0]
    frow = lax.broadcasted_iota(jnp.int32, (LANES, tq), 0)
    zero = jnp.zeros_like(qt)
    qm = [jnp.concatenate([jnp.where(frow < HEAD_DIM, qt, zero), brows_ref[0]], axis=0),
          jnp.concatenate([jnp.where(frow >= HEAD_DIM, qt, zero), brows_ref[1]], axis=0)]

    acc_ref[...] = jnp.zeros_like(acc_ref)
    carry_ref[...] = jnp.zeros_like(carry_ref)

    def tiles(kbs, diag):
        if diag:
            key = lax.broadcasted_iota(jnp.int32, (tk, tq), 0)
            qry = lax.broadcasted_iota(jnp.int32, (tk, tq), 1)
            vis = key < qry
        kblks = [jnp.concatenate([k_ref[0, pl.ds(pl.multiple_of(kb * tk, tk), tk), :], ones_ref[...]], axis=1)
                 for kb in kbs]
        vcat = jnp.concatenate([vt_ref[0, kb] for kb in kbs], axis=1)
        ts = [[jnp.dot(kblk, qm[h], preferred_element_type=F32) for kblk in kblks]
              for h in range(HEADS_PER_BLOCK)]
        sps = [[_softplus2(t) for t in th] for th in ts]
        if diag:
            spms = [[jnp.where(vis, sp, 0.0) for sp in sh] for sh in sps]
        else:
            spms = sps
        sexs = [[jnp.dot(tri_ref[...], spm.astype(BF16), preferred_element_type=F32) for spm in sh] for sh in spms]
        for h in range(HEADS_PER_BLOCK):
            ws = []
            carry = carry_ref[h]
            for t, sp, spm, sex in zip(ts[h], sps[h], spms[h], sexs[h]):
                w = jnp.exp2((t - sp) - sex - carry)
                if diag:
                    w = jnp.where(vis, w, 0.0)
                ws.append(w.astype(BF16))
                carry = carry + (sex[0:1, :] + spm[0:1, :])
            carry_ref[h] = carry
            acc_ref[h] += jnp.dot(vcat, jnp.concatenate(ws, axis=0), preferred_element_type=F32)

    tiles([qi], True)

    def body(i, c):
        kb = qi - 1 - KEY_UNROLL * i
        tiles([kb - u for u in range(KEY_UNROLL)], False)
        return c

    lax.fori_loop(0, qi // KEY_UNROLL, body, 0)
    for rem in range(1, KEY_UNROLL):
        @pl.when(qi % KEY_UNROLL >= rem)
        def _():
            tiles([(qi % KEY_UNROLL) - rem], False)

    o_t = jnp.where(frow < HEAD_DIM, acc_ref[0], acc_ref[1])
    o_ref[0] = o_t.T


def _attn_sample_kernel(pt_ref, q_ref, kn_ref, vn_ref, bias_ref, tri_ref, *rest, nt, n_heads, g_pages):
    del pt_ref
    kp_refs = rest[0:g_pages]
    vp_refs = rest[g_pages:2 * g_pages]
    o_ref, qbd_ref, acc_ref, carry_ref = rest[2 * g_pages:]
    s = pl.program_id(1)
    rows = nt * n_heads
    d_attn = n_heads * HEAD_DIM
    page = kn_ref.shape[1]

    head_row = lax.broadcasted_iota(jnp.int32, (n_heads, d_attn), 0)
    head_lane = lax.broadcasted_iota(jnp.int32, (n_heads, d_attn), 1) // HEAD_DIM
    own = head_row == head_lane

    def weights(t, vis):
        sp = _softplus2(t)
        spm = sp if vis is None else jnp.where(vis, sp, 0.0)
        sex = jnp.dot(spm.astype(BF16), tri_ref[...], preferred_element_type=F32)
        w = jnp.exp2((t - sp) - sex - carry_ref[...])
        if vis is not None:
            w = jnp.where(vis, w, 0.0)
        carry_ref[...] += sex[:, 0:1] + spm[:, 0:1]
        return w.astype(BF16)

    @pl.when(s == 0)
    def _():
        q = q_ref[0]
        parts = [jnp.where(own, jnp.broadcast_to(q[t:t + 1, :], (n_heads, d_attn)), 0.0) for t in range(nt)]
        qbd_ref[...] = jnp.concatenate(parts, axis=0).astype(BF16)
        acc_ref[...] = jnp.zeros_like(acc_ref)
        carry_ref[...] = jnp.zeros_like(carry_ref)
        r_step = lax.broadcasted_iota(jnp.int32, (rows, page), 0) // n_heads
        col = lax.broadcasted_iota(jnp.int32, (rows, page), 1)
        t = lax.dot_general(qbd_ref[...], kn_ref[0], NT_DIMS, preferred_element_type=F32) + bias_ref[...]
        w = weights(t, col < r_step)
        acc_ref[...] += jnp.dot(w, vn_ref[0], preferred_element_type=F32)

    on_lanes = lambda refs: jnp.concatenate([r[...].reshape(d_attn, page).astype(BF16) for r in refs], axis=1)
    to_rows = lambda a: jnp.concatenate([a[:, g * page:(g + 1) * page] for g in range(g_pages)], axis=0)
    t_all = jnp.dot(qbd_ref[...], on_lanes(kp_refs), preferred_element_type=F32) + bias_ref[...]
    sp_all = _softplus2(t_all)
    sp_rows = to_rows(sp_all)
    sex_rows = jnp.dot(sp_rows.astype(BF16), tri_ref[...], preferred_element_type=F32)
    tot_rows = sex_rows[:, 0:1] + sp_rows[:, 0:1]
    carry = carry_ref[...]
    carries = []
    for g in range(g_pages):
        carries.append(carry)
        carry = carry + tot_rows[g * rows:(g + 1) * rows, :]
    carry_ref[...] = carry
    w_rows = jnp.exp2(to_rows(t_all - sp_all) - sex_rows - jnp.concatenate(carries, axis=0))
    w_all = jnp.concatenate([w_rows[g * rows:(g + 1) * rows, :] for g in range(g_pages)], axis=1).astype(BF16)
    acc_ref[...] += lax.dot_general(w_all, on_lanes(vp_refs), NT_DIMS, preferred_element_type=F32)

    @pl.when(s == pl.num_programs(1) - 1)
    def _():
        acc = acc_ref[...]
        outs = [jnp.sum(jnp.where(own, acc[t * n_heads:(t + 1) * n_heads, :], 0.0), axis=0, keepdims=True)
                for t in range(nt)]
        o_ref[0] = jnp.concatenate(outs, axis=0)


def _full(shape):
    n = len(shape)
    return pl.BlockSpec(shape, lambda *_: (0,) * n)


def _tri_lower(n):
    j = lax.broadcasted_iota(jnp.int32, (n, n), 0)
    s = lax.broadcasted_iota(jnp.int32, (n, n), 1)
    return (j > s).astype(BF16)


def _row(a):
    return a.reshape(1, -1).astype(F32)


def _proj_prompt(x, lw, n_real, qscale):
    bsz, lp, d_model = x.shape
    d_attn, d_conv = lw["d_attn"], lw["d_conv"]
    tm = ROW_TILE
    nblk = lp // tm
    tail_block = (n_real - 1) // tm
    tail_off = n_real - tail_block * tm
    assert tail_off % 8 == 0 and tail_off >= 8
    kern = functools.partial(_proj_prompt_kernel, tm=tm, d_attn=d_attn, d_conv=d_conv, tail_block=tail_block,
                             tail_off=tail_off, qscale=qscale)
    rows = lambda w: pl.BlockSpec((1, tm, w), lambda b, i: (b, i, 0))
    cols = lambda w: pl.BlockSpec((1, w, tm), lambda b, i: (b, 0, i))
    w_in = lw["w_in"]
    w_qkv_t = w_in[:, 0:3 * d_attn].T
    w_k = w_in[:, d_attn:2 * d_attn]
    w_rest = w_in[:, 3 * d_attn:]
    return pl.pallas_call(
        kern,
        grid=(bsz, nblk),
        in_specs=[rows(d_model), _full(w_qkv_t.shape), _full(w_k.shape), _full(w_rest.shape), _full((1, d_model)),
                  _full((CONV_WIDTH, d_conv)), _full((1, d_conv)), _full((1, d_conv)), _full((1, d_conv)),
                  _full((d_conv, d_conv)), _full((1, d_conv))],
        out_specs=[cols(d_attn), rows(d_attn),
                   pl.BlockSpec((1, 1, d_attn, tm), lambda b, i: (b, i, 0, 0)),
                   cols(d_attn), cols(d_attn), rows(d_attn), rows(d_conv),
                   pl.BlockSpec((1, HALO, d_conv), lambda b, i: (b, 0, 0))],
        out_shape=[jax.ShapeDtypeStruct((bsz, d_attn, lp), BF16),
                   jax.ShapeDtypeStruct((bsz, lp, d_attn), BF16),
                   jax.ShapeDtypeStruct((bsz, nblk, d_attn, tm), BF16),
                   jax.ShapeDtypeStruct((bsz, d_attn, n_real), F32),
                   jax.ShapeDtypeStruct((bsz, d_attn, n_real), F32),
                   jax.ShapeDtypeStruct((bsz, lp, d_attn), BF16),
                   jax.ShapeDtypeStruct((bsz, lp, d_conv), BF16),
                   jax.ShapeDtypeStruct((bsz, HALO, d_conv), F32)],
        scratch_shapes=[pltpu.VMEM((tm + HALO, d_conv), F32), pltpu.VMEM((8, tm + HALO, d_conv), F32),
                        pltpu.VMEM((tm, d_conv), F32)],
        compiler_params=pltpu.CompilerParams(dimension_semantics=("arbitrary", "arbitrary"),
                                             vmem_limit_bytes=VMEM_LIMIT),
        name="proj_prompt",
    )(x, w_qkv_t, w_k, w_rest, lw["g_pre"], lw["conv_w"], lw["conv_b"], lw["ln_g"], lw["ln_b"], lw["w_pw2"],
      lw["g_conv_out"])


def _proj_sample(x, state_t, lw, nb, nt, qscale):
    rows, d_model = x.shape
    d_attn, d_conv = lw["d_attn"], lw["d_conv"]
    kern = functools.partial(_proj_sample_kernel, nb=nb, nt=nt, d_attn=d_attn, d_conv=d_conv, qscale=qscale)
    f = lambda w, dt: jax.ShapeDtypeStruct((rows, w), dt)
    return pl.pallas_call(
        kern,
        out_shape=[f(d_attn, F32), f(d_attn, F32), f(d_attn, F32), f(d_attn, BF16), f(d_conv, BF16), f(d_conv, F32)],
        compiler_params=pltpu.CompilerParams(vmem_limit_bytes=VMEM_LIMIT),
        name="proj_sample",
    )(x, state_t, lw["w_in"], lw["g_pre"], lw["conv_w"], lw["conv_b"], lw["ln_g"], lw["ln_b"], lw["w_pw2"],
      lw["g_conv_out"])


def _out_proj(oa, ga, mc, x, lw, tm):
    bsz, lp, d_model = x.shape
    d_attn, d_conv = lw["d_attn"], lw["d_conv"]
    blk = lambda w: pl.BlockSpec((1, tm, w), lambda b, i: (b, i, 0))
    return pl.pallas_call(
        functools.partial(_out_kernel, d_attn=d_attn),
        grid=(bsz, lp // tm),
        in_specs=[blk(d_attn), blk(d_attn), blk(d_conv), blk(d_model), _full(lw["w_out"].shape),
                  _full((1, d_attn)), _full((1, d_model))],
        out_specs=blk(d_model),
        out_shape=jax.ShapeDtypeStruct((bsz, lp, d_model), F32),
        compiler_params=pltpu.CompilerParams(dimension_semantics=("arbitrary", "arbitrary"),
                                             vmem_limit_bytes=VMEM_LIMIT),
        name="out_proj",
    )(oa, ga, mc, x, lw["w_out"], lw["g_attn_out"], lw["g_post"])


BIAS_PIECES = 3


def _bias_rows(bias2, tq):
    pieces = []
    rest = bias2
    for _ in range(BIAS_PIECES):
        piece = rest.astype(BF16)
        pieces.append(piece)
        rest = rest - piece.astype(F32)
    rows = jnp.stack(pieces, axis=1)
    rows = jnp.pad(rows, ((0, 0), (0, LANES - BIAS_PIECES)))
    return jnp.broadcast_to(rows[:, :, None], rows.shape + (tq,))


def _ones_cols(tk):
    col = lax.broadcasted_iota(jnp.int32, (tk, LANES), 1)
    return (col < BIAS_PIECES).astype(BF16)


def _attn_prompt(qt, kb, vt, bias2, tri_t):
    bsz, d_attn, lp = qt.shape
    tq = ATT_TILE
    nkb = lp // tq
    n_pairs = d_attn // LANES
    return pl.pallas_call(
        _attn_prompt_kernel,
        grid=(bsz, n_pairs, lp // tq),
        in_specs=[pl.BlockSpec((1, LANES, tq), lambda b, hp, qi: (b, hp, qi)),
                  pl.BlockSpec((HEADS_PER_BLOCK, LANES, tq), lambda b, hp, qi: (hp, 0, 0)),
                  pl.BlockSpec((1, lp, LANES), lambda b, hp, qi: (b, 0, hp)),
                  _full((tq, LANES)),
                  pl.BlockSpec((1, nkb, LANES, tq), lambda b, hp, qi: (b, 0, hp, 0)),
                  _full((tq, tq))],
        out_specs=pl.BlockSpec((1, tq, LANES), lambda b, hp, qi: (b, qi, hp)),
        out_shape=jax.ShapeDtypeStruct((bsz, lp, d_attn), F32),
        scratch_shapes=[pltpu.VMEM((HEADS_PER_BLOCK, LANES, tq), F32), pltpu.VMEM((HEADS_PER_BLOCK, 1, tq), F32)],
        compiler_params=pltpu.CompilerParams(dimension_semantics=("arbitrary", "arbitrary", "arbitrary"),
                                             vmem_limit_bytes=VMEM_LIMIT),
        name="attn_prompt",
    )(qt, _bias_rows(bias2, tq), kb, _ones_cols(tq), vt, tri_t)


def _attn_sample(q, k_new, v_new, bias_rows, tri, cache_kt, cache_vt, layer, pt_flat, n_pages, n_heads):
    nb, nt, d_attn = q.shape
    page = cache_kt.shape[-1]
    g_pages = math.gcd(PAGES_PER_STEP, n_pages)
    n_steps = n_pages // g_pages
    rows = nt * n_heads

    def page_spec(g):
        return pl.BlockSpec((None, None, n_heads, HEAD_DIM, page),
                            lambda b, s, pt: (layer, pt[b * n_pages + n_pages - 1 - (s * g_pages + g)], 0, 0, 0))

    per_seq = lambda shape: pl.BlockSpec(shape, lambda b, s, pt: (b, 0, 0))
    const = lambda shape: pl.BlockSpec(shape, lambda b, s, pt: (0, 0))
    grid_spec = pltpu.PrefetchScalarGridSpec(
        num_scalar_prefetch=1,
        grid=(nb, n_steps),
        in_specs=[per_seq((1, nt, d_attn)), per_seq((1, page, d_attn)), per_seq((1, page, d_attn)),
                  const((rows, 1)), const((page, page))]
                 + [page_spec(g) for g in range(g_pages)] + [page_spec(g) for g in range(g_pages)],
        out_specs=per_seq((1, nt, d_attn)),
        scratch_shapes=[pltpu.VMEM((rows, d_attn), BF16), pltpu.VMEM((rows, d_attn), F32),
                        pltpu.VMEM((rows, 1), F32)],
    )
    kern = functools.partial(_attn_sample_kernel, nt=nt, n_heads=n_heads, g_pages=g_pages)
    return pl.pallas_call(
        kern,
        grid_spec=grid_spec,
        out_shape=jax.ShapeDtypeStruct((nb, nt, d_attn), F32),
        compiler_params=pltpu.CompilerParams(dimension_semantics=("arbitrary", "arbitrary"),
                                             vmem_limit_bytes=VMEM_LIMIT),
        name="attn_sample",
    )(pt_flat, q, k_new, v_new, bias_rows, tri, *([cache_kt] * g_pages), *([cache_vt] * g_pages))


def kernel(x_prompt, x_sample, cache_k, cache_v, state_conv, page_table, meta_tokens,
           w_in, g_pre, g_post, sb_bias, g_attn_out, conv_w, conv_b, ln_g, ln_b, w_pw2, g_conv_out, w_out):
    depth = w_in.shape[0]
    bp, seq, d_model = x_prompt.shape
    nb, nt, _ = x_sample.shape
    n_pool, page, n_heads, head_dim = cache_k.shape[1:]
    assert head_dim == HEAD_DIM
    d_attn = n_heads * head_dim
    d_conv = w_pw2.shape[-1]
    n_pages = page_table.shape[1]
    n_real = N_META + seq
    lp = -(-n_real // ATT_TILE) * ATT_TILE
    qscale = LOG2E * HEAD_DIM ** -0.5

    meta = jnp.broadcast_to(meta_tokens.astype(x_prompt.dtype)[None], (bp, N_META, d_model))
    h_p = jnp.concatenate([meta, x_prompt, jnp.zeros((bp, lp - n_real, d_model), x_prompt.dtype)], axis=1)
    h_s = x_sample.transpose(1, 0, 2).reshape(nt * nb, d_model)
    pt_flat = page_table.reshape(-1).astype(jnp.int32)
    tri_p = _tri_lower(ATT_TILE).T
    tri_s = _tri_lower(page)
    cache_kt = cache_k.transpose(0, 1, 3, 4, 2)
    cache_vt = cache_v.transpose(0, 1, 3, 4, 2)

    kp_l, vp_l, cp_l, ks_l, vs_l, cs_l = [], [], [], [], [], []
    for l in range(depth):
        lw = dict(d_attn=d_attn, d_conv=d_conv, w_in=w_in[l].astype(BF16), g_pre=_row(g_pre[l]),
                  g_post=_row(g_post[l]), g_attn_out=_row(g_attn_out[l]), conv_w=conv_w[l].astype(F32),
                  conv_b=_row(conv_b[l]), ln_g=_row(ln_g[l]), ln_b=_row(ln_b[l]), w_pw2=w_pw2[l].astype(BF16),
                  g_conv_out=_row(g_conv_out[l]), w_out=w_out[l].astype(BF16))
        bias2 = sb_bias[l].astype(F32) * LOG2E

        qt, kb, vt, kft, vft, ga, mc, tail = _proj_prompt(h_p, lw, n_real, qscale)
        oa = _attn_prompt(qt, kb, vt, bias2, tri_p)
        h_p = _out_proj(oa, ga, mc, h_p, lw, ROW_TILE)
        to_heads = lambda a: a.reshape(bp, n_heads, head_dim, n_real).transpose(0, 3, 1, 2)
        kp_l.append(to_heads(kft))
        vp_l.append(to_heads(vft))
        cp_l.append(tail[:, HALO - (CONV_WIDTH - 1):])

        state_t = state_conv[l].astype(F32).transpose(1, 0, 2)
        qs, ksf, vsf, gas, mcs, us = _proj_sample(h_s, state_t, lw, nb, nt, qscale)
        to_seq = lambda a: a.reshape(nt, nb, -1).transpose(1, 0, 2)
        pad_page = lambda a: jnp.pad(to_seq(a).astype(BF16), ((0, 0), (0, page - nt), (0, 0)))
        bias_rows = jnp.tile(bias2, nt).reshape(nt * n_heads, 1)
        oas = _attn_sample(to_seq(qs), pad_page(ksf), pad_page(vsf), bias_rows, tri_s, cache_kt, cache_vt, l,
                           pt_flat, n_pages, n_heads)
        oas = oas.transpose(1, 0, 2).reshape(1, nt * nb, d_attn)
        h_s = _out_proj(oas, gas[None], mcs[None], h_s[None], lw, nt * nb)[0]
        ks_l.append(to_seq(ksf).reshape(nb, nt, n_heads, head_dim))
        vs_l.append(to_seq(vsf).reshape(nb, nt, n_heads, head_dim))
        cs_l.append(jnp.concatenate([state_conv[l][:, nt:], to_seq(us).astype(state_conv.dtype)], axis=1))

    y_prompt = h_p[:, N_META:n_real]
    y_sample = h_s.reshape(nt, nb, d_model).transpose(1, 0, 2)
    return (y_prompt, y_sample, jnp.stack(kp_l), jnp.stack(vp_l), jnp.stack(cp_l),
            jnp.stack(ks_l), jnp.stack(vs_l), jnp.stack(cs_l))
```

```python
import functools
import math

import jax
import jax.numpy as jnp
from jax import lax
from jax.experimental import pallas as pl
from jax.experimental.pallas import tpu as pltpu

N_META = 16
HEAD_DIM = 64
CONV_WIDTH = 31
EPS = 1e-6
LOG2E = 1.4426950408889634

LANES = 128
HEADS_PER_BLOCK = LANES // HEAD_DIM
ATT_TILE = 256
PAIRS_PER_STEP = 2
BIAS_PIECES = 3
ROW_TILE = ATT_TILE
HALO = 32
CONV_CHUNK = 64
PAGES_PER_STEP = 16
VMEM_LIMIT = 56 * 1024 * 1024

F32 = jnp.float32
BF16 = jnp.bfloat16
NT_DIMS = (((1,), (1,)), ((), ()))


def _rms_scale(x):
    return lax.rsqrt(jnp.mean(x * x, axis=-1, keepdims=True) + EPS)


def _silu(x):
    return x * jax.nn.sigmoid(x)


def _softplus2(t):
    return jnp.maximum(t, 0.0) + jnp.log2(1.0 + jnp.exp2(-jnp.abs(t)))


def _normed(x, gpre_ref):
    return (x * _rms_scale(x) * gpre_ref[...]).astype(BF16)


def _conv_tail(c, gate_c, lng_ref, lnb_ref, wpw2_ref, gco_ref):
    mu = jnp.mean(c, axis=-1, keepdims=True)
    cc = c - mu
    y = cc * lax.rsqrt(jnp.mean(cc * cc, axis=-1, keepdims=True) + EPS) * lng_ref[...] + lnb_ref[...]
    p = jnp.dot(_silu(y).astype(BF16), wpw2_ref[...], preferred_element_type=F32)
    return p * _rms_scale(p) * gco_ref[...] * _silu(gate_c)


def _proj_prompt_kernel(x_ref, wqkvt_ref, wk_ref, wrest_ref, gpre_ref, convw_ref, convb_ref, lng_ref, lnb_ref,
                        wpw2_ref, gco_ref,
                        qt_ref, kb_ref, vt_ref, kft_ref, vft_ref, ga_ref, mc_ref, tail_ref,
                        ubuf, ush, cbuf, *, tm, d_attn, d_conv, tail_block, tail_off, qscale):
    i = pl.program_id(1)

    @pl.when(i == 0)
    def _():
        ubuf[0:HALO, :] = jnp.zeros((HALO, d_conv), F32)

    hn = _normed(x_ref[0], gpre_ref)
    qkv_t = lax.dot_general(wqkvt_ref[...], hn, NT_DIMS, preferred_element_type=F32)
    qt_ref[0, 0] = (qkv_t[0:d_attn] * qscale).astype(BF16)
    k_t = qkv_t[d_attn:2 * d_attn]
    v_t = qkv_t[2 * d_attn:3 * d_attn]
    kft_ref[0] = k_t
    vft_ref[0] = v_t
    vt_ref[0, 0] = v_t.astype(BF16)
    kb_ref[0] = jnp.dot(hn, wk_ref[...], preferred_element_type=F32).astype(BF16)

    rest = lambda c: jnp.dot(hn, wrest_ref[:, c * d_conv:(c + 1) * d_conv], preferred_element_type=F32)
    ga_ref[0] = _silu(rest(0)).astype(BF16)

    ubuf[HALO:HALO + tm, :] = rest(1) * jax.nn.sigmoid(rest(2))
    span = tm + HALO - 8
    for r in range(1, 8):
        ush[r, 0:span, :] = ubuf[r:r + span, :]

    def chunk(ci, carry):
        r0 = pl.multiple_of(ci * CONV_CHUNK, CONV_CHUNK)
        acc = jnp.broadcast_to(convb_ref[...], (CONV_CHUNK, d_conv))
        for r in range(8):
            taps = [(j, (HALO - (CONV_WIDTH - 1) + j) // 8) for j in range(CONV_WIDTH)
                    if (HALO - (CONV_WIDTH - 1) + j) % 8 == r]
            a_lo = min(a for _, a in taps)
            a_hi = max(a for _, a in taps)
            rows = CONV_CHUNK + 8 * (a_hi - a_lo)
            if r == 0:
                win = ubuf[pl.ds(r0 + 8 * a_lo, rows), :]
            else:
                win = ush[r, pl.ds(r0 + 8 * a_lo, rows), :]
            for j, a in taps:
                acc = acc + convw_ref[j:j + 1, :] * win[8 * (a - a_lo):8 * (a - a_lo) + CONV_CHUNK, :]
        cbuf[pl.ds(r0, CONV_CHUNK), :] = acc
        return carry

    lax.fori_loop(0, tm // CONV_CHUNK, chunk, 0)
    mc_ref[0] = _conv_tail(cbuf[...], rest(3), lng_ref, lnb_ref, wpw2_ref, gco_ref).astype(BF16)

    @pl.when(i == tail_block)
    def _():
        tail_ref[0] = ubuf[tail_off:tail_off + HALO, :]

    ubuf[0:HALO, :] = ubuf[tm:tm + HALO, :]


def _proj_sample_kernel(x_ref, state_ref, win_ref, gpre_ref, convw_ref, convb_ref, lng_ref, lnb_ref, wpw2_ref,
                        gco_ref, q_ref, kf_ref, vf_ref, ga_ref, mc_ref, u_ref, *, nb, nt, d_attn, d_conv, qscale):
    hn = _normed(x_ref[...], gpre_ref)
    cols = [0, d_attn, 2 * d_attn, 3 * d_attn, 4 * d_attn, 4 * d_attn + d_conv, 4 * d_attn + 2 * d_conv,
            4 * d_attn + 3 * d_conv]
    q, k, v, gate_a, glu_val, glu_gate, gate_c = [
        jnp.dot(hn, win_ref[:, cols[i]:cols[i + 1]], preferred_element_type=F32) for i in range(7)]
    q_ref[...] = q * qscale
    kf_ref[...] = k
    vf_ref[...] = v
    ga_ref[...] = _silu(gate_a).astype(BF16)
    u = glu_val * jax.nn.sigmoid(glu_gate)
    u_ref[...] = u
    hist = CONV_WIDTH - 1
    cs = []
    for t in range(nt):
        acc = jnp.broadcast_to(convb_ref[...], (nb, d_conv))
        for j in range(CONV_WIDTH):
            p = t + j
            src = state_ref[p] if p < hist else u[(p - hist) * nb:(p - hist + 1) * nb, :]
            acc = acc + convw_ref[j:j + 1, :] * src
        cs.append(acc)
    c = jnp.concatenate(cs, axis=0)
    mc_ref[...] = _conv_tail(c, gate_c, lng_ref, lnb_ref, wpw2_ref, gco_ref).astype(BF16)


def _out_kernel(oa_ref, ga_ref, mc_ref, x_ref, wout_ref, gattn_ref, gpost_ref, y_ref, *, d_attn):
    oa = oa_ref[0]
    ma = (oa * _rms_scale(oa) * gattn_ref[...] * ga_ref[0].astype(F32)).astype(BF16)
    y = jnp.dot(ma, wout_ref[0:d_attn, :], preferred_element_type=F32)
    y = y + jnp.dot(mc_ref[0], wout_ref[d_attn:, :], preferred_element_type=F32)
    y_ref[0] = x_ref[0] + y * _rms_scale(y) * gpost_ref[...]


def _attn_prompt_kernel(qt_ref, brows_ref, k_ref, ones_ref, vt_ref, tri_ref, o_ref, acc_ref, carry_ref, t_even, t_odd,
                        w_even, w_odd):
    qi = pl.program_id(2)
    tq = tk = ATT_TILE
    n_heads = PAIRS_PER_STEP * HEADS_PER_BLOCK
    heads = range(n_heads)
    top = lax.broadcasted_iota(jnp.int32, (LANES, tq), 0) < HEAD_DIM
    pair_rows = lambda a, p: a[p * LANES:(p + 1) * LANES]
    qt = qt_ref[0, 0]
    qm = []
    for h in heads:
        qp = pair_rows(qt, h // HEADS_PER_BLOCK)
        zero = jnp.zeros_like(qp)
        own = jnp.where(top, qp, zero) if h % HEADS_PER_BLOCK == 0 else jnp.where(top, zero, qp)
        qm.append(jnp.concatenate([own, brows_ref[h]], axis=0))

    def key_blocks(kb):
        rows = k_ref[0, pl.ds(pl.multiple_of(kb * tk, tk), tk), :]
        return [jnp.concatenate([rows[:, p * LANES:(p + 1) * LANES], ones_ref[...]], axis=1)
                for p in range(PAIRS_PER_STEP)]

    def logits(ks, h):
        return jnp.dot(ks[h // HEADS_PER_BLOCK], qm[h], preferred_element_type=F32)

    def suffix_sums(spm):
        return jnp.dot(tri_ref[...], spm.astype(BF16), preferred_element_type=F32)

    def emit_weights(h, t, sp, spm, sex, vis, w_wr):
        w = jnp.exp2((t - sp) - sex - carry_ref[h])
        if vis is not None:
            w = jnp.where(vis, w, 0.0)
        w_wr[h] = w.astype(BF16)
        carry_ref[h] += sex[0:1, :] + spm[0:1, :]

    def accumulate(kb, w_rd):
        v = vt_ref[0, kb]
        for h in heads:
            acc_ref[h] += jnp.dot(pair_rows(v, h // HEADS_PER_BLOCK), w_rd[h], preferred_element_type=F32)

    acc_ref[...] = jnp.zeros_like(acc_ref)
    carry_ref[...] = jnp.zeros_like(carry_ref)
    key = lax.broadcasted_iota(jnp.int32, (tk, tq), 0)
    qry = lax.broadcasted_iota(jnp.int32, (tk, tq), 1)
    vis = key < qry
    ks_diag = key_blocks(qi)
    ks_next = key_blocks(jnp.maximum(qi - 1, 0))
    ts = [logits(ks_diag, h) for h in heads]
    sps = [_softplus2(t) for t in ts]
    spms = [jnp.where(vis, sp, 0.0) for sp in sps]
    sexs = [suffix_sums(spm) for spm in spms]
    for h in heads:
        t_even[h] = logits(ks_next, h)
    for h in heads:
        emit_weights(h, ts[h], sps[h], spms[h], sexs[h], vis, w_even)

    def body(i, c):
        kb = qi - 1 - i
        for parity, t_rd, t_wr, w_rd, w_wr in ((0, t_even, t_odd, w_even, w_odd), (1, t_odd, t_even, w_odd, w_even)):
            @pl.when(i % 2 == parity)
            def _():
                accumulate(kb + 1, w_rd)
                ks = key_blocks(jnp.maximum(kb - 1, 0))
                ts, sps, sexs = [], [], []
                for h in heads:
                    ts.append(t_rd[h])
                    sps.append(_softplus2(ts[h]))
                    sexs.append(suffix_sums(sps[h]))
                    t_wr[h] = logits(ks, h)
                for h in heads:
                    emit_weights(h, ts[h], sps[h], sps[h], sexs[h], None, w_wr)
        return c

    lax.fori_loop(0, qi, body, 0)
    for parity, w_rd in ((0, w_even), (1, w_odd)):
        @pl.when(qi % 2 == parity)
        def _():
            accumulate(0, w_rd)
    o_t = jnp.concatenate([jnp.where(top, acc_ref[HEADS_PER_BLOCK * p], acc_ref[HEADS_PER_BLOCK * p + 1])
                           for p in range(PAIRS_PER_STEP)], axis=0)
    o_ref[0] = o_t.T


def _attn_sample_kernel(pt_ref, q_ref, kn_ref, vn_ref, bias_ref, tri_ref, *rest, nt, n_heads, g_pages):
    del pt_ref
    kp_refs = rest[0:g_pages]
    vp_refs = rest[g_pages:2 * g_pages]
    o_ref, qbd_ref, acc_ref, carry_ref = rest[2 * g_pages:]
    s = pl.program_id(1)
    rows = nt * n_heads
    d_attn = n_heads * HEAD_DIM
    page = kn_ref.shape[1]

    head_row = lax.broadcasted_iota(jnp.int32, (n_heads, d_attn), 0)
    head_lane = lax.broadcasted_iota(jnp.int32, (n_heads, d_attn), 1) // HEAD_DIM
    own = head_row == head_lane

    def weights(t, vis):
        sp = _softplus2(t)
        spm = sp if vis is None else jnp.where(vis, sp, 0.0)
        sex = jnp.dot(spm.astype(BF16), tri_ref[...], preferred_element_type=F32)
        w = jnp.exp2((t - sp) - sex - carry_ref[...])
        if vis is not None:
            w = jnp.where(vis, w, 0.0)
        carry_ref[...] += sex[:, 0:1] + spm[:, 0:1]
        return w.astype(BF16)

    @pl.when(s == 0)
    def _():
        q = q_ref[0]
        parts = [jnp.where(own, jnp.broadcast_to(q[t:t + 1, :], (n_heads, d_attn)), 0.0) for t in range(nt)]
        qbd_ref[...] = jnp.concatenate(parts, axis=0).astype(BF16)
        acc_ref[...] = jnp.zeros_like(acc_ref)
        carry_ref[...] = jnp.zeros_like(carry_ref)
        r_step = lax.broadcasted_iota(jnp.int32, (rows, page), 0) // n_heads
        col = lax.broadcasted_iota(jnp.int32, (rows, page), 1)
        t = lax.dot_general(qbd_ref[...], kn_ref[0], NT_DIMS, preferred_element_type=F32) + bias_ref[...]
        w = weights(t, col < r_step)
        acc_ref[...] += jnp.dot(w, vn_ref[0], preferred_element_type=F32)

    on_lanes = lambda refs: jnp.concatenate([r[...].reshape(d_attn, page).astype(BF16) for r in refs], axis=1)
    to_rows = lambda a: jnp.concatenate([a[:, g * page:(g + 1) * page] for g in range(g_pages)], axis=0)
    t_all = jnp.dot(qbd_ref[...], on_lanes(kp_refs), preferred_element_type=F32) + bias_ref[...]
    sp_all = _softplus2(t_all)
    sp_rows = to_rows(sp_all)
    sex_rows = jnp.dot(sp_rows.astype(BF16), tri_ref[...], preferred_element_type=F32)
    tot_rows = sex_rows[:, 0:1] + sp_rows[:, 0:1]
    carry = carry_ref[...]
    carries = []
    for g in range(g_pages):
        carries.append(carry)
        carry = carry + tot_rows[g * rows:(g + 1) * rows, :]
    carry_ref[...] = carry
    w_rows = jnp.exp2(to_rows(t_all - sp_all) - sex_rows - jnp.concatenate(carries, axis=0))
    w_all = jnp.concatenate([w_rows[g * rows:(g + 1) * rows, :] for g in range(g_pages)], axis=1).astype(BF16)
    acc_ref[...] += lax.dot_general(w_all, on_lanes(vp_refs), NT_DIMS, preferred_element_type=F32)

    @pl.when(s == pl.num_programs(1) - 1)
    def _():
        acc = acc_ref[...]
        outs = [jnp.sum(jnp.where(own, acc[t * n_heads:(t + 1) * n_heads, :], 0.0), axis=0, keepdims=True)
                for t in range(nt)]
        o_ref[0] = jnp.concatenate(outs, axis=0)


def _full(shape):
    n = len(shape)
    return pl.BlockSpec(shape, lambda *_: (0,) * n)


def _tri_lower(n):
    j = lax.broadcasted_iota(jnp.int32, (n, n), 0)
    s = lax.broadcasted_iota(jnp.int32, (n, n), 1)
    return (j > s).astype(BF16)


def _row(a):
    return a.reshape(1, -1).astype(F32)


def _proj_prompt(x, lw, n_real, qscale):
    bsz, lp, d_model = x.shape
    d_attn, d_conv = lw["d_attn"], lw["d_conv"]
    tm = ROW_TILE
    nblk = lp // tm
    tail_block = (n_real - 1) // tm
    tail_off = n_real - tail_block * tm
    assert tail_off % 8 == 0 and tail_off >= 8
    kern = functools.partial(_proj_prompt_kernel, tm=tm, d_attn=d_attn, d_conv=d_conv, tail_block=tail_block,
                             tail_off=tail_off, qscale=qscale)
    rows = lambda w: pl.BlockSpec((1, tm, w), lambda b, i: (b, i, 0))
    cols = lambda w: pl.BlockSpec((1, w, tm), lambda b, i: (b, 0, i))
    w_in = lw["w_in"]
    w_qkv_t = w_in[:, 0:3 * d_attn].T
    w_k = w_in[:, d_attn:2 * d_attn]
    w_rest = w_in[:, 3 * d_attn:]
    return pl.pallas_call(
        kern,
        grid=(bsz, nblk),
        in_specs=[rows(d_model), _full(w_qkv_t.shape), _full(w_k.shape), _full(w_rest.shape), _full((1, d_model)),
                  _full((CONV_WIDTH, d_conv)), _full((1, d_conv)), _full((1, d_conv)), _full((1, d_conv)),
                  _full((d_conv, d_conv)), _full((1, d_conv))],
        out_specs=[pl.BlockSpec((1, 1, d_attn, tm), lambda b, i: (b, i, 0, 0)), rows(d_attn),
                   pl.BlockSpec((1, 1, d_attn, tm), lambda b, i: (b, i, 0, 0)),
                   cols(d_attn), cols(d_attn), rows(d_attn), rows(d_conv),
                   pl.BlockSpec((1, HALO, d_conv), lambda b, i: (b, 0, 0))],
        out_shape=[jax.ShapeDtypeStruct((bsz, nblk, d_attn, tm), BF16),
                   jax.ShapeDtypeStruct((bsz, lp, d_attn), BF16),
                   jax.ShapeDtypeStruct((bsz, nblk, d_attn, tm), BF16),
                   jax.ShapeDtypeStruct((bsz, d_attn, n_real), F32),
                   jax.ShapeDtypeStruct((bsz, d_attn, n_real), F32),
                   jax.ShapeDtypeStruct((bsz, lp, d_attn), BF16),
                   jax.ShapeDtypeStruct((bsz, lp, d_conv), BF16),
                   jax.ShapeDtypeStruct((bsz, HALO, d_conv), F32)],
        scratch_shapes=[pltpu.VMEM((tm + HALO, d_conv), F32), pltpu.VMEM((8, tm + HALO, d_conv), F32),
                        pltpu.VMEM((tm, d_conv), F32)],
        compiler_params=pltpu.CompilerParams(dimension_semantics=("arbitrary", "arbitrary"),
                                             vmem_limit_bytes=VMEM_LIMIT),
        name="proj_prompt",
    )(x, w_qkv_t, w_k, w_rest, lw["g_pre"], lw["conv_w"], lw["conv_b"], lw["ln_g"], lw["ln_b"], lw["w_pw2"],
      lw["g_conv_out"])


def _proj_sample(x, state_t, lw, nb, nt, qscale):
    rows, d_model = x.shape
    d_attn, d_conv = lw["d_attn"], lw["d_conv"]
    kern = functools.partial(_proj_sample_kernel, nb=nb, nt=nt, d_attn=d_attn, d_conv=d_conv, qscale=qscale)
    f = lambda w, dt: jax.ShapeDtypeStruct((rows, w), dt)
    return pl.pallas_call(
        kern,
        out_shape=[f(d_attn, F32), f(d_attn, F32), f(d_attn, F32), f(d_attn, BF16), f(d_conv, BF16), f(d_conv, F32)],
        compiler_params=pltpu.CompilerParams(vmem_limit_bytes=VMEM_LIMIT),
        name="proj_sample",
    )(x, state_t, lw["w_in"], lw["g_pre"], lw["conv_w"], lw["conv_b"], lw["ln_g"], lw["ln_b"], lw["w_pw2"],
      lw["g_conv_out"])


def _out_proj(oa, ga, mc, x, lw, tm):
    bsz, lp, d_model = x.shape
    d_attn, d_conv = lw["d_attn"], lw["d_conv"]
    blk = lambda w: pl.BlockSpec((1, tm, w), lambda b, i: (b, i, 0))
    return pl.pallas_call(
        functools.partial(_out_kernel, d_attn=d_attn),
        grid=(bsz, lp // tm),
        in_specs=[blk(d_attn), blk(d_attn), blk(d_conv), blk(d_model), _full(lw["w_out"].shape),
                  _full((1, d_attn)), _full((1, d_model))],
        out_specs=blk(d_model),
        out_shape=jax.ShapeDtypeStruct((bsz, lp, d_model), F32),
        compiler_params=pltpu.CompilerParams(dimension_semantics=("arbitrary", "arbitrary"),
                                             vmem_limit_bytes=VMEM_LIMIT),
        name="out_proj",
    )(oa, ga, mc, x, lw["w_out"], lw["g_attn_out"], lw["g_post"])


def _bias_rows(bias2, tq):
    pieces = []
    rest = bias2
    for _ in range(BIAS_PIECES):
        piece = rest.astype(BF16)
        pieces.append(piece)
        rest = rest - piece.astype(F32)
    rows = jnp.stack(pieces, axis=1)
    rows = jnp.pad(rows, ((0, 0), (0, LANES - BIAS_PIECES)))
    return jnp.broadcast_to(rows[:, :, None], rows.shape + (tq,))


def _ones_cols(tk):
    col = lax.broadcasted_iota(jnp.int32, (tk, LANES), 1)
    return (col < BIAS_PIECES).astype(BF16)


def _attn_prompt(qt, kb, vt, bias2, tri_t):
    bsz, nq, d_attn, tq = qt.shape
    lp = nq * tq
    width = PAIRS_PER_STEP * LANES
    n_heads = PAIRS_PER_STEP * HEADS_PER_BLOCK
    return pl.pallas_call(
        _attn_prompt_kernel,
        grid=(bsz, d_attn // width, nq),
        in_specs=[pl.BlockSpec((1, 1, width, tq), lambda b, g, qi: (b, qi, g, 0)),
                  pl.BlockSpec((n_heads, LANES, tq), lambda b, g, qi: (g, 0, 0)),
                  pl.BlockSpec((1, lp, width), lambda b, g, qi: (b, 0, g)),
                  _full((tq, LANES)),
                  pl.BlockSpec((1, nq, width, tq), lambda b, g, qi: (b, 0, g, 0)),
                  _full((tq, tq))],
        out_specs=pl.BlockSpec((1, tq, width), lambda b, g, qi: (b, qi, g)),
        out_shape=jax.ShapeDtypeStruct((bsz, lp, d_attn), F32),
        scratch_shapes=[pltpu.VMEM((n_heads, LANES, tq), F32),
                        pltpu.VMEM((n_heads, 1, tq), F32),
                        pltpu.VMEM((n_heads, tq, tq), F32),
                        pltpu.VMEM((n_heads, tq, tq), F32),
                        pltpu.VMEM((n_heads, tq, tq), BF16),
                        pltpu.VMEM((n_heads, tq, tq), BF16)],
        compiler_params=pltpu.CompilerParams(dimension_semantics=("arbitrary", "arbitrary", "arbitrary"),
                                             vmem_limit_bytes=VMEM_LIMIT),
        name="attn_prompt",
    )(qt, _bias_rows(bias2, tq), kb, _ones_cols(tq), vt, tri_t)


def _attn_sample(q, k_new, v_new, bias_rows, tri, cache_kt, cache_vt, layer, pt_flat, n_pages, n_heads):
    nb, nt, d_attn = q.shape
    page = cache_kt.shape[-1]
    g_pages = math.gcd(PAGES_PER_STEP, n_pages)
    n_steps = n_pages // g_pages
    rows = nt * n_heads

    def page_spec(g):
        return pl.BlockSpec((None, None, n_heads, HEAD_DIM, page),
                            lambda b, s, pt: (layer, pt[b * n_pages + n_pages - 1 - (s * g_pages + g)], 0, 0, 0))

    per_seq = lambda shape: pl.BlockSpec(shape, lambda b, s, pt: (b, 0, 0))
    const = lambda shape: pl.BlockSpec(shape, lambda b, s, pt: (0, 0))
    grid_spec = pltpu.PrefetchScalarGridSpec(
        num_scalar_prefetch=1,
        grid=(nb, n_steps),
        in_specs=[per_seq((1, nt, d_attn)), per_seq((1, page, d_attn)), per_seq((1, page, d_attn)),
                  const((rows, 1)), const((page, page))]
                 + [page_spec(g) for g in range(g_pages)] + [page_spec(g) for g in range(g_pages)],
        out_specs=per_seq((1, nt, d_attn)),
        scratch_shapes=[pltpu.VMEM((rows, d_attn), BF16), pltpu.VMEM((rows, d_attn), F32),
                        pltpu.VMEM((rows, 1), F32)],
    )
    kern = functools.partial(_attn_sample_kernel, nt=nt, n_heads=n_heads, g_pages=g_pages)
    return pl.pallas_call(
        kern,
        grid_spec=grid_spec,
        out_shape=jax.ShapeDtypeStruct((nb, nt, d_attn), F32),
        compiler_params=pltpu.CompilerParams(dimension_semantics=("arbitrary", "arbitrary"),
                                             vmem_limit_bytes=VMEM_LIMIT),
        name="attn_sample",
    )(pt_flat, q, k_new, v_new, bias_rows, tri, *([cache_kt] * g_pages), *([cache_vt] * g_pages))


def kernel(x_prompt, x_sample, cache_k, cache_v, state_conv, page_table, meta_tokens,
           w_in, g_pre, g_post, sb_bias, g_attn_out, conv_w, conv_b, ln_g, ln_b, w_pw2, g_conv_out, w_out):
    depth = w_in.shape[0]
    bp, seq, d_model = x_prompt.shape
    nb, nt, _ = x_sample.shape
    n_pool, page, n_heads, head_dim = cache_k.shape[1:]
    assert head_dim == HEAD_DIM
    d_attn = n_heads * head_dim
    d_conv = w_pw2.shape[-1]
    n_pages = page_table.shape[1]
    n_real = N_META + seq
    lp = -(-n_real // ATT_TILE) * ATT_TILE
    qscale = LOG2E * HEAD_DIM ** -0.5

    meta = jnp.broadcast_to(meta_tokens.astype(x_prompt.dtype)[None], (bp, N_META, d_model))
    h_p = jnp.concatenate([meta, x_prompt, jnp.zeros((bp, lp - n_real, d_model), x_prompt.dtype)], axis=1)
    h_s = x_sample.transpose(1, 0, 2).reshape(nt * nb, d_model)
    pt_flat = page_table.reshape(-1).astype(jnp.int32)
    tri_p = _tri_lower(ATT_TILE).T
    tri_s = _tri_lower(page)
    cache_kt = cache_k.transpose(0, 1, 3, 4, 2)
    cache_vt = cache_v.transpose(0, 1, 3, 4, 2)

    kp_l, vp_l, cp_l, ks_l, vs_l, cs_l = [], [], [], [], [], []
    for l in range(depth):
        lw = dict(d_attn=d_attn, d_conv=d_conv, w_in=w_in[l].astype(BF16), g_pre=_row(g_pre[l]),
                  g_post=_row(g_post[l]), g_attn_out=_row(g_attn_out[l]), conv_w=conv_w[l].astype(F32),
                  conv_b=_row(conv_b[l]), ln_g=_row(ln_g[l]), ln_b=_row(ln_b[l]), w_pw2=w_pw2[l].astype(BF16),
                  g_conv_out=_row(g_conv_out[l]), w_out=w_out[l].astype(BF16))
        bias2 = sb_bias[l].astype(F32) * LOG2E

        qt, kb, vt, kft, vft, ga, mc, tail = _proj_prompt(h_p, lw, n_real, qscale)
        oa = _attn_prompt(qt, kb, vt, bias2, tri_p)
        h_p = _out_proj(oa, ga, mc, h_p, lw, ROW_TILE)
        to_heads = lambda a: a.reshape(bp, n_heads, head_dim, n_real).transpose(0, 3, 1, 2)
        kp_l.append(to_heads(kft))
        vp_l.append(to_heads(vft))
        cp_l.append(tail[:, HALO - (CONV_WIDTH - 1):])

        state_t = state_conv[l].astype(F32).transpose(1, 0, 2)
        qs, ksf, vsf, gas, mcs, us = _proj_sample(h_s, state_t, lw, nb, nt, qscale)
        to_seq = lambda a: a.reshape(nt, nb, -1).transpose(1, 0, 2)
        pad_page = lambda a: jnp.pad(to_seq(a).astype(BF16), ((0, 0), (0, page - nt), (0, 0)))
        bias_rows = jnp.tile(bias2, nt).reshape(nt * n_heads, 1)
        oas = _attn_sample(to_seq(qs), pad_page(ksf), pad_page(vsf), bias_rows, tri_s, cache_kt, cache_vt, l,
                           pt_flat, n_pages, n_heads)
        oas = oas.transpose(1, 0, 2).reshape(1, nt * nb, d_attn)
        h_s = _out_proj(oas, gas[None], mcs[None], h_s[None], lw, nt * nb)[0]
        ks_l.append(to_seq(ksf).reshape(nb, nt, n_heads, head_dim))
        vs_l.append(to_seq(vsf).reshape(nb, nt, n_heads, head_dim))
        cs_l.append(jnp.concatenate([state_conv[l][:, nt:], to_seq(us).astype(state_conv.dtype)], axis=1))

    y_prompt = h_p[:, N_META:n_real]
    y_sample = h_s.reshape(nt, nb, d_model).transpose(1, 0, 2)
    return (y_prompt, y_sample, jnp.stack(kp_l), jnp.stack(vp_l), jnp.stack(cp_l),
            jnp.stack(ks_l), jnp.stack(vs_l), jnp.stack(cs_l))
```

```python
import functools
import math

import jax
import jax.numpy as jnp
from jax import lax
from jax.experimental import pallas as pl
from jax.experimental.pallas import tpu as pltpu

N_META = 16
HEAD_DIM = 64
CONV_WIDTH = 31
EPS = 1e-6
LOG2E = 1.4426950408889634
SIGN_BIT = -2 ** 31

LANES = 128
HEADS_PER_BLOCK = LANES // HEAD_DIM
ATT_TILE = 256
PAIRS_PER_STEP = 4
BIAS_PIECES = 3
ROW_TILE = ATT_TILE
OUT_TILES = 3
HALO = 32
CONV_CHUNK = 64
PAGES_PER_STEP = 16
VMEM_LIMIT = 56 * 1024 * 1024

F32 = jnp.float32
BF16 = jnp.bfloat16
NT_DIMS = (((1,), (1,)), ((), ()))


def _rms_scale(x):
    return lax.rsqrt(jnp.mean(x * x, axis=-1, keepdims=True) + EPS)


def _silu(x):
    return x * jax.nn.sigmoid(x)


def _softplus2(t):
    neg_abs = lax.bitcast_convert_type(lax.bitcast_convert_type(t, jnp.int32) | jnp.int32(SIGN_BIT), F32)
    return jnp.maximum(t, 0.0) + jnp.log2(1.0 + jnp.exp2(neg_abs))


def _normed(x, gpre_ref):
    return (x * _rms_scale(x) * gpre_ref[...]).astype(BF16)


def _conv_tail(c, gate_c, lng_ref, lnb_ref, wpw2_ref, gco_ref):
    mu = jnp.mean(c, axis=-1, keepdims=True)
    cc = c - mu
    y = cc * lax.rsqrt(jnp.mean(cc * cc, axis=-1, keepdims=True) + EPS) * lng_ref[...] + lnb_ref[...]
    p = jnp.dot(_silu(y).astype(BF16), wpw2_ref[...], preferred_element_type=F32)
    return p * _rms_scale(p) * gco_ref[...] * _silu(gate_c)


def _proj_prompt_kernel(x_ref, wqkvt_ref, wrest_ref, gpre_ref, convw_ref, convb_ref, lng_ref, lnb_ref,
                        wpw2_ref, gco_ref,
                        qt_ref, kb_ref, vt_ref, kft_ref, vft_ref, ga_ref, mc_ref, tail_ref,
                        ubuf, ush, cbuf, *, tm, d_attn, d_conv, tail_block, tail_off, qscale):
    i = pl.program_id(1)

    @pl.when(i == 0)
    def _():
        ubuf[0:HALO, :] = jnp.zeros((HALO, d_conv), F32)

    hn = _normed(x_ref[0], gpre_ref)
    qkv_t = lax.dot_general(wqkvt_ref[...], hn, NT_DIMS, preferred_element_type=F32)
    qt_ref[0, 0] = (qkv_t[0:d_attn] * qscale).astype(BF16)
    k_t = qkv_t[d_attn:2 * d_attn]
    v_t = qkv_t[2 * d_attn:3 * d_attn]
    kft_ref[0] = k_t
    vft_ref[0] = v_t
    vt_ref[0, 0] = v_t.astype(BF16)
    kb_ref[0] = k_t.T.astype(BF16)

    rest = lambda c: jnp.dot(hn, wrest_ref[:, c * d_conv:(c + 1) * d_conv], preferred_element_type=F32)
    ga_ref[0] = _silu(rest(0)).astype(BF16)

    ubuf[HALO:HALO + tm, :] = rest(1) * jax.nn.sigmoid(rest(2))
    span = tm + HALO - 8
    for r in range(1, 8):
        ush[r, 0:span, :] = ubuf[r:r + span, :]

    def chunk(ci, carry):
        r0 = pl.multiple_of(ci * CONV_CHUNK, CONV_CHUNK)
        acc = jnp.broadcast_to(convb_ref[...], (CONV_CHUNK, d_conv))
        for r in range(8):
            taps = [(j, (HALO - (CONV_WIDTH - 1) + j) // 8) for j in range(CONV_WIDTH)
                    if (HALO - (CONV_WIDTH - 1) + j) % 8 == r]
            a_lo = min(a for _, a in taps)
            a_hi = max(a for _, a in taps)
            rows = CONV_CHUNK + 8 * (a_hi - a_lo)
            if r == 0:
                win = ubuf[pl.ds(r0 + 8 * a_lo, rows), :]
            else:
                win = ush[r, pl.ds(r0 + 8 * a_lo, rows), :]
            for j, a in taps:
                acc = acc + convw_ref[j:j + 1, :] * win[8 * (a - a_lo):8 * (a - a_lo) + CONV_CHUNK, :]
        cbuf[pl.ds(r0, CONV_CHUNK), :] = acc
        return carry

    lax.fori_loop(0, tm // CONV_CHUNK, chunk, 0)
    mc_ref[0] = _conv_tail(cbuf[...], rest(3), lng_ref, lnb_ref, wpw2_ref, gco_ref).astype(BF16)

    @pl.when(i == tail_block)
    def _():
        tail_ref[0] = ubuf[tail_off:tail_off + HALO, :]

    ubuf[0:HALO, :] = ubuf[tm:tm + HALO, :]


def _proj_sample_kernel(x_ref, state_ref, win_ref, gpre_ref, convw_ref, convb_ref, lng_ref, lnb_ref, wpw2_ref,
                        gco_ref, q_ref, kf_ref, vf_ref, ga_ref, mc_ref, u_ref, *, nb, nt, d_attn, d_conv, qscale):
    hn = _normed(x_ref[...], gpre_ref)
    cols = [0, d_attn, 2 * d_attn, 3 * d_attn, 4 * d_attn, 4 * d_attn + d_conv, 4 * d_attn + 2 * d_conv,
            4 * d_attn + 3 * d_conv]
    q, k, v, gate_a, glu_val, glu_gate, gate_c = [
        jnp.dot(hn, win_ref[:, cols[i]:cols[i + 1]], preferred_element_type=F32) for i in range(7)]
    q_ref[...] = q * qscale
    kf_ref[...] = k
    vf_ref[...] = v
    ga_ref[...] = _silu(gate_a).astype(BF16)
    u = glu_val * jax.nn.sigmoid(glu_gate)
    u_ref[...] = u
    hist = CONV_WIDTH - 1
    cs = []
    for t in range(nt):
        acc = jnp.broadcast_to(convb_ref[...], (nb, d_conv))
        for j in range(CONV_WIDTH):
            p = t + j
            src = state_ref[p] if p < hist else u[(p - hist) * nb:(p - hist + 1) * nb, :]
            acc = acc + convw_ref[j:j + 1, :] * src
        cs.append(acc)
    c = jnp.concatenate(cs, axis=0)
    mc_ref[...] = _conv_tail(c, gate_c, lng_ref, lnb_ref, wpw2_ref, gco_ref).astype(BF16)


def _out_kernel(oa_ref, ga_ref, mc_ref, x_ref, wout_ref, gattn_ref, gpost_ref, y_ref, *, d_attn):
    oa = oa_ref[0]
    ma = (oa * _rms_scale(oa) * gattn_ref[...] * ga_ref[0].astype(F32)).astype(BF16)
    y = jnp.dot(ma, wout_ref[0:d_attn, :], preferred_element_type=F32)
    y = y + jnp.dot(mc_ref[0], wout_ref[d_attn:, :], preferred_element_type=F32)
    y_ref[0] = x_ref[0] + y * _rms_scale(y) * gpost_ref[...]


def _attn_prompt_kernel(qt_ref, brows_ref, k_ref, ones_ref, vt_ref, tri_ref, o_ref, acc_ref, carry_ref, t_even, t_odd,
                        w_even, w_odd):
    qi = pl.program_id(2)
    tq = tk = ATT_TILE
    n_heads = PAIRS_PER_STEP * HEADS_PER_BLOCK
    heads = range(n_heads)
    top = lax.broadcasted_iota(jnp.int32, (LANES, tq), 0) < HEAD_DIM
    pair_rows = lambda a, p: a[p * LANES:(p + 1) * LANES]
    qt = qt_ref[0, 0]
    qm = []
    for h in heads:
        qp = pair_rows(qt, h // HEADS_PER_BLOCK)
        zero = jnp.zeros_like(qp)
        own = jnp.where(top, qp, zero) if h % HEADS_PER_BLOCK == 0 else jnp.where(top, zero, qp)
        qm.append(jnp.concatenate([own, brows_ref[h]], axis=0))

    def key_blocks(kb):
        rows = k_ref[0, pl.ds(pl.multiple_of(kb * tk, tk), tk), :]
        return [jnp.concatenate([rows[:, p * LANES:(p + 1) * LANES], ones_ref[...]], axis=1)
                for p in range(PAIRS_PER_STEP)]

    def logits(ks, h):
        return jnp.dot(ks[h // HEADS_PER_BLOCK], qm[h], preferred_element_type=F32)

    def suffix_sums(spm):
        return jnp.dot(tri_ref[...], spm.astype(BF16), preferred_element_type=F32)

    def emit_weights(h, t, sp, spm, sex, vis, w_wr):
        w = jnp.exp2((t - sp) - sex - carry_ref[h])
        if vis is not None:
            w = jnp.where(vis, w, 0.0)
        w_wr[h] = w.astype(BF16)
        carry_ref[h] += sex[0:1, :] + spm[0:1, :]

    def accumulate(kb, w_rd):
        v = vt_ref[0, kb]
        for h in heads:
            acc_ref[h] += jnp.dot(pair_rows(v, h // HEADS_PER_BLOCK), w_rd[h], preferred_element_type=F32)

    acc_ref[...] = jnp.zeros_like(acc_ref)
    carry_ref[...] = jnp.zeros_like(carry_ref)
    key = lax.broadcasted_iota(jnp.int32, (tk, tq), 0)
    qry = lax.broadcasted_iota(jnp.int32, (tk, tq), 1)
    vis = key < qry
    ks_diag = key_blocks(qi)
    ks_next = key_blocks(jnp.maximum(qi - 1, 0))
    ts = [logits(ks_diag, h) for h in heads]
    sps = [_softplus2(t) for t in ts]
    spms = [jnp.where(vis, sp, 0.0) for sp in sps]
    sexs = [suffix_sums(spm) for spm in spms]
    for h in heads:
        t_even[h] = logits(ks_next, h)
    for h in heads:
        emit_weights(h, ts[h], sps[h], spms[h], sexs[h], vis, w_even)

    def body(i, c):
        kb = qi - 1 - i
        for parity, t_rd, t_wr, w_rd, w_wr in ((0, t_even, t_odd, w_even, w_odd), (1, t_odd, t_even, w_odd, w_even)):
            @pl.when(i % 2 == parity)
            def _():
                accumulate(kb + 1, w_rd)
                ks = key_blocks(jnp.maximum(kb - 1, 0))
                ts, sps, sexs = [], [], []
                for h in heads:
                    ts.append(t_rd[h])
                    sps.append(_softplus2(ts[h]))
                    sexs.append(suffix_sums(sps[h]))
                    t_wr[h] = logits(ks, h)
                for h in heads:
                    emit_weights(h, ts[h], sps[h], sps[h], sexs[h], None, w_wr)
        return c

    lax.fori_loop(0, qi, body, 0)
    for parity, w_rd in ((0, w_even), (1, w_odd)):
        @pl.when(qi % 2 == parity)
        def _():
            accumulate(0, w_rd)
    o_t = jnp.concatenate([jnp.where(top, acc_ref[HEADS_PER_BLOCK * p], acc_ref[HEADS_PER_BLOCK * p + 1])
                           for p in range(PAIRS_PER_STEP)], axis=0)
    o_ref[0] = o_t.T


def _attn_sample_kernel(pt_ref, q_ref, kn_ref, vn_ref, bias_ref, tri_ref, *rest, nt, n_heads, g_pages):
    del pt_ref
    kp_refs = rest[0:g_pages]
    vp_refs = rest[g_pages:2 * g_pages]
    o_ref, qbd_ref, acc_ref, carry_ref = rest[2 * g_pages:]
    s = pl.program_id(1)
    rows = nt * n_heads
    d_attn = n_heads * HEAD_DIM
    page = kn_ref.shape[1]

    head_row = lax.broadcasted_iota(jnp.int32, (n_heads, d_attn), 0)
    head_lane = lax.broadcasted_iota(jnp.int32, (n_heads, d_attn), 1) // HEAD_DIM
    own = head_row == head_lane

    def weights(t, vis):
        sp = _softplus2(t)
        spm = sp if vis is None else jnp.where(vis, sp, 0.0)
        sex = jnp.dot(spm.astype(BF16), tri_ref[...], preferred_element_type=F32)
        w = jnp.exp2((t - sp) - sex - carry_ref[...])
        if vis is not None:
            w = jnp.where(vis, w, 0.0)
        carry_ref[...] += sex[:, 0:1] + spm[:, 0:1]
        return w.astype(BF16)

    @pl.when(s == 0)
    def _():
        q = q_ref[0]
        parts = [jnp.where(own, jnp.broadcast_to(q[t:t + 1, :], (n_heads, d_attn)), 0.0) for t in range(nt)]
        qbd_ref[...] = jnp.concatenate(parts, axis=0).astype(BF16)
        acc_ref[...] = jnp.zeros_like(acc_ref)
        carry_ref[...] = jnp.zeros_like(carry_ref)
        r_step = lax.broadcasted_iota(jnp.int32, (rows, page), 0) // n_heads
        col = lax.broadcasted_iota(jnp.int32, (rows, page), 1)
        t = lax.dot_general(qbd_ref[...], kn_ref[0], NT_DIMS, preferred_element_type=F32) + bias_ref[...]
        w = weights(t, col < r_step)
        acc_ref[...] += jnp.dot(w, vn_ref[0], preferred_element_type=F32)

    on_lanes = lambda refs: jnp.concatenate([r[...].reshape(d_attn, page).astype(BF16) for r in refs], axis=1)
    to_rows = lambda a: jnp.concatenate([a[:, g * page:(g + 1) * page] for g in range(g_pages)], axis=0)
    t_all = jnp.dot(qbd_ref[...], on_lanes(kp_refs), preferred_element_type=F32) + bias_ref[...]
    sp_all = _softplus2(t_all)
    sp_rows = to_rows(sp_all)
    sex_rows = jnp.dot(sp_rows.astype(BF16), tri_ref[...], preferred_element_type=F32)
    tot_rows = sex_rows[:, 0:1] + sp_rows[:, 0:1]
    carry = carry_ref[...]
    carries = []
    for g in range(g_pages):
        carries.append(carry)
        carry = carry + tot_rows[g * rows:(g + 1) * rows, :]
    carry_ref[...] = carry
    w_rows = jnp.exp2(to_rows(t_all - sp_all) - sex_rows - jnp.concatenate(carries, axis=0))
    w_all = jnp.concatenate([w_rows[g * rows:(g + 1) * rows, :] for g in range(g_pages)], axis=1).astype(BF16)
    acc_ref[...] += lax.dot_general(w_all, on_lanes(vp_refs), NT_DIMS, preferred_element_type=F32)

    @pl.when(s == pl.num_programs(1) - 1)
    def _():
        acc = acc_ref[...]
        outs = [jnp.sum(jnp.where(own, acc[t * n_heads:(t + 1) * n_heads, :], 0.0), axis=0, keepdims=True)
                for t in range(nt)]
        o_ref[0] = jnp.concatenate(outs, axis=0)


def _full(shape):
    n = len(shape)
    return pl.BlockSpec(shape, lambda *_: (0,) * n)


def _tri_lower(n):
    j = lax.broadcasted_iota(jnp.int32, (n, n), 0)
    s = lax.broadcasted_iota(jnp.int32, (n, n), 1)
    return (j > s).astype(BF16)


def _row(a):
    return a.reshape(1, -1).astype(F32)


def _proj_prompt(x, lw, n_real, qscale):
    bsz, lp, d_model = x.shape
    d_attn, d_conv = lw["d_attn"], lw["d_conv"]
    tm = ROW_TILE
    nblk = lp // tm
    tail_block = (n_real - 1) // tm
    tail_off = n_real - tail_block * tm
    assert tail_off % 8 == 0 and tail_off >= 8
    kern = functools.partial(_proj_prompt_kernel, tm=tm, d_attn=d_attn, d_conv=d_conv, tail_block=tail_block,
                             tail_off=tail_off, qscale=qscale)
    rows = lambda w: pl.BlockSpec((1, tm, w), lambda b, i: (b, i, 0))
    cols = lambda w: pl.BlockSpec((1, w, tm), lambda b, i: (b, 0, i))
    w_in = lw["w_in"]
    w_qkv_t = w_in[:, 0:3 * d_attn].T
    w_rest = w_in[:, 3 * d_attn:]
    return pl.pallas_call(
        kern,
        grid=(bsz, nblk),
        in_specs=[rows(d_model), _full(w_qkv_t.shape), _full(w_rest.shape), _full((1, d_model)),
                  _full((CONV_WIDTH, d_conv)), _full((1, d_conv)), _full((1, d_conv)), _full((1, d_conv)),
                  _full((d_conv, d_conv)), _full((1, d_conv))],
        out_specs=[pl.BlockSpec((1, 1, d_attn, tm), lambda b, i: (b, i, 0, 0)), rows(d_attn),
                   pl.BlockSpec((1, 1, d_attn, tm), lambda b, i: (b, i, 0, 0)),
                   cols(d_attn), cols(d_attn), rows(d_attn), rows(d_conv),
                   pl.BlockSpec((1, HALO, d_conv), lambda b, i: (b, 0, 0))],
        out_shape=[jax.ShapeDtypeStruct((bsz, nblk, d_attn, tm), BF16),
                   jax.ShapeDtypeStruct((bsz, lp, d_attn), BF16),
                   jax.ShapeDtypeStruct((bsz, nblk, d_attn, tm), BF16),
                   jax.ShapeDtypeStruct((bsz, d_attn, n_real), F32),
                   jax.ShapeDtypeStruct((bsz, d_attn, n_real), F32),
                   jax.ShapeDtypeStruct((bsz, lp, d_attn), BF16),
                   jax.ShapeDtypeStruct((bsz, lp, d_conv), BF16),
                   jax.ShapeDtypeStruct((bsz, HALO, d_conv), F32)],
        scratch_shapes=[pltpu.VMEM((tm + HALO, d_conv), F32), pltpu.VMEM((8, tm + HALO, d_conv), F32),
                        pltpu.VMEM((tm, d_conv), F32)],
        compiler_params=pltpu.CompilerParams(dimension_semantics=("arbitrary", "arbitrary"),
                                             vmem_limit_bytes=VMEM_LIMIT),
        name="proj_prompt",
    )(x, w_qkv_t, w_rest, lw["g_pre"], lw["conv_w"], lw["conv_b"], lw["ln_g"], lw["ln_b"], lw["w_pw2"],
      lw["g_conv_out"])


def _proj_sample(x, state_t, lw, nb, nt, qscale):
    rows, d_model = x.shape
    d_attn, d_conv = lw["d_attn"], lw["d_conv"]
    kern = functools.partial(_proj_sample_kernel, nb=nb, nt=nt, d_attn=d_attn, d_conv=d_conv, qscale=qscale)
    f = lambda w, dt: jax.ShapeDtypeStruct((rows, w), dt)
    return pl.pallas_call(
        kern,
        out_shape=[f(d_attn, F32), f(d_attn, F32), f(d_attn, F32), f(d_attn, BF16), f(d_conv, BF16), f(d_conv, F32)],
        compiler_params=pltpu.CompilerParams(vmem_limit_bytes=VMEM_LIMIT),
        name="proj_sample",
    )(x, state_t, lw["w_in"], lw["g_pre"], lw["conv_w"], lw["conv_b"], lw["ln_g"], lw["ln_b"], lw["w_pw2"],
      lw["g_conv_out"])


def _out_proj(oa, ga, mc, x, lw, tm):
    bsz, lp, d_model = x.shape
    d_attn, d_conv = lw["d_attn"], lw["d_conv"]
    blk = lambda w: pl.BlockSpec((1, tm, w), lambda b, i: (b, i, 0))
    return pl.pallas_call(
        functools.partial(_out_kernel, d_attn=d_attn),
        grid=(bsz, lp // tm),
        in_specs=[blk(d_attn), blk(d_attn), blk(d_conv), blk(d_model), _full(lw["w_out"].shape),
                  _full((1, d_attn)), _full((1, d_model))],
        out_specs=blk(d_model),
        out_shape=jax.ShapeDtypeStruct((bsz, lp, d_model), F32),
        compiler_params=pltpu.CompilerParams(dimension_semantics=("arbitrary", "arbitrary"),
                                             vmem_limit_bytes=VMEM_LIMIT),
        name="out_proj",
    )(oa, ga, mc, x, lw["w_out"], lw["g_attn_out"], lw["g_post"])


def _bias_rows(bias2, tq):
    pieces = []
    rest = bias2
    for _ in range(BIAS_PIECES):
        piece = rest.astype(BF16)
        pieces.append(piece)
        rest = rest - piece.astype(F32)
    rows = jnp.stack(pieces, axis=1)
    rows = jnp.pad(rows, ((0, 0), (0, LANES - BIAS_PIECES)))
    return jnp.broadcast_to(rows[:, :, None], rows.shape + (tq,))


def _ones_cols(tk):
    col = lax.broadcasted_iota(jnp.int32, (tk, LANES), 1)
    return (col < BIAS_PIECES).astype(BF16)


def _attn_prompt(qt, kb, vt, bias2, tri_t):
    bsz, nq, d_attn, tq = qt.shape
    lp = nq * tq
    width = PAIRS_PER_STEP * LANES
    n_heads = PAIRS_PER_STEP * HEADS_PER_BLOCK
    return pl.pallas_call(
        _attn_prompt_kernel,
        grid=(bsz, d_attn // width, nq),
        in_specs=[pl.BlockSpec((1, 1, width, tq), lambda b, g, qi: (b, qi, g, 0)),
                  pl.BlockSpec((n_heads, LANES, tq), lambda b, g, qi: (g, 0, 0)),
                  pl.BlockSpec((1, lp, width), lambda b, g, qi: (b, 0, g), pipeline_mode=pl.Buffered(1)),
                  _full((tq, LANES)),
                  pl.BlockSpec((1, nq, width, tq), lambda b, g, qi: (b, 0, g, 0), pipeline_mode=pl.Buffered(1)),
                  _full((tq, tq))],
        out_specs=pl.BlockSpec((1, tq, width), lambda b, g, qi: (b, qi, g)),
        out_shape=jax.ShapeDtypeStruct((bsz, lp, d_attn), F32),
        scratch_shapes=[pltpu.VMEM((n_heads, LANES, tq), F32),
                        pltpu.VMEM((n_heads, 1, tq), F32),
                        pltpu.VMEM((n_heads, tq, tq), F32),
                        pltpu.VMEM((n_heads, tq, tq), F32),
                        pltpu.VMEM((n_heads, tq, tq), BF16),
                        pltpu.VMEM((n_heads, tq, tq), BF16)],
        compiler_params=pltpu.CompilerParams(dimension_semantics=("arbitrary", "arbitrary", "arbitrary"),
                                             vmem_limit_bytes=VMEM_LIMIT),
        name="attn_prompt",
    )(qt, _bias_rows(bias2, tq), kb, _ones_cols(tq), vt, tri_t)


def _attn_sample(q, k_new, v_new, bias_rows, tri, cache_kt, cache_vt, layer, pt_flat, n_pages, n_heads):
    nb, nt, d_attn = q.shape
    page = cache_kt.shape[-1]
    g_pages = math.gcd(PAGES_PER_STEP, n_pages)
    n_steps = n_pages // g_pages
    rows = nt * n_heads

    def page_spec(g):
        return pl.BlockSpec((None, None, n_heads, HEAD_DIM, page),
                            lambda b, s, pt: (layer, pt[b * n_pages + n_pages - 1 - (s * g_pages + g)], 0, 0, 0))

    per_seq = lambda shape: pl.BlockSpec(shape, lambda b, s, pt: (b, 0, 0))
    const = lambda shape: pl.BlockSpec(shape, lambda b, s, pt: (0, 0))
    grid_spec = pltpu.PrefetchScalarGridSpec(
        num_scalar_prefetch=1,
        grid=(nb, n_steps),
        in_specs=[per_seq((1, nt, d_attn)), per_seq((1, page, d_attn)), per_seq((1, page, d_attn)),
                  const((rows, 1)), const((page, page))]
                 + [page_spec(g) for g in range(g_pages)] + [page_spec(g) for g in range(g_pages)],
        out_specs=per_seq((1, nt, d_attn)),
        scratch_shapes=[pltpu.VMEM((rows, d_attn), BF16), pltpu.VMEM((rows, d_attn), F32),
                        pltpu.VMEM((rows, 1), F32)],
    )
    kern = functools.partial(_attn_sample_kernel, nt=nt, n_heads=n_heads, g_pages=g_pages)
    return pl.pallas_call(
        kern,
        grid_spec=grid_spec,
        out_shape=jax.ShapeDtypeStruct((nb, nt, d_attn), F32),
        compiler_params=pltpu.CompilerParams(dimension_semantics=("arbitrary", "arbitrary"),
                                             vmem_limit_bytes=VMEM_LIMIT),
        name="attn_sample",
    )(pt_flat, q, k_new, v_new, bias_rows, tri, *([cache_kt] * g_pages), *([cache_vt] * g_pages))


def kernel(x_prompt, x_sample, cache_k, cache_v, state_conv, page_table, meta_tokens,
           w_in, g_pre, g_post, sb_bias, g_attn_out, conv_w, conv_b, ln_g, ln_b, w_pw2, g_conv_out, w_out):
    depth = w_in.shape[0]
    bp, seq, d_model = x_prompt.shape
    nb, nt, _ = x_sample.shape
    n_pool, page, n_heads, head_dim = cache_k.shape[1:]
    assert head_dim == HEAD_DIM
    d_attn = n_heads * head_dim
    d_conv = w_pw2.shape[-1]
    n_pages = page_table.shape[1]
    n_real = N_META + seq
    lp = -(-n_real // ATT_TILE) * ATT_TILE
    qscale = LOG2E * HEAD_DIM ** -0.5

    meta = jnp.broadcast_to(meta_tokens.astype(x_prompt.dtype)[None], (bp, N_META, d_model))
    h_p = jnp.concatenate([meta, x_prompt, jnp.zeros((bp, lp - n_real, d_model), x_prompt.dtype)], axis=1)
    h_s = x_sample.transpose(1, 0, 2).reshape(nt * nb, d_model)
    pt_flat = page_table.reshape(-1).astype(jnp.int32)
    tri_p = _tri_lower(ATT_TILE).T
    tri_s = _tri_lower(page)
    cache_kt = cache_k.transpose(0, 1, 3, 4, 2)
    cache_vt = cache_v.transpose(0, 1, 3, 4, 2)

    kp_l, vp_l, cp_l, ks_l, vs_l, cs_l = [], [], [], [], [], []
    for l in range(depth):
        lw = dict(d_attn=d_attn, d_conv=d_conv, w_in=w_in[l].astype(BF16), g_pre=_row(g_pre[l]),
                  g_post=_row(g_post[l]), g_attn_out=_row(g_attn_out[l]), conv_w=conv_w[l].astype(F32),
                  conv_b=_row(conv_b[l]), ln_g=_row(ln_g[l]), ln_b=_row(ln_b[l]), w_pw2=w_pw2[l].astype(BF16),
                  g_conv_out=_row(g_conv_out[l]), w_out=w_out[l].astype(BF16))
        bias2 = sb_bias[l].astype(F32) * LOG2E

        qt, kb, vt, kft, vft, ga, mc, tail = _proj_prompt(h_p, lw, n_real, qscale)
        oa = _attn_prompt(qt, kb, vt, bias2, tri_p)
        out_tiles = max(d for d in range(1, OUT_TILES + 1) if (lp // ROW_TILE) % d == 0)
        h_p = _out_proj(oa, ga, mc, h_p, lw, out_tiles * ROW_TILE)
        to_heads = lambda a: a.reshape(bp, n_heads, head_dim, n_real).transpose(0, 3, 1, 2)
        kp_l.append(to_heads(kft))
        vp_l.append(to_heads(vft))
        cp_l.append(tail[:, HALO - (CONV_WIDTH - 1):])

        state_t = state_conv[l].astype(F32).transpose(1, 0, 2)
        qs, ksf, vsf, gas, mcs, us = _proj_sample(h_s, state_t, lw, nb, nt, qscale)
        to_seq = lambda a: a.reshape(nt, nb, -1).transpose(1, 0, 2)
        pad_page = lambda a: jnp.pad(to_seq(a).astype(BF16), ((0, 0), (0, page - nt), (0, 0)))
        bias_rows = jnp.tile(bias2, nt).reshape(nt * n_heads, 1)
        oas = _attn_sample(to_seq(qs), pad_page(ksf), pad_page(vsf), bias_rows, tri_s, cache_kt, cache_vt, l,
                           pt_flat, n_pages, n_heads)
        oas = oas.transpose(1, 0, 2).reshape(1, nt * nb, d_attn)
        h_s = _out_proj(oas, gas[None], mcs[None], h_s[None], lw, nt * nb)[0]
        ks_l.append(to_seq(ksf).reshape(nb, nt, n_heads, head_dim))
        vs_l.append(to_seq(vsf).reshape(nb, nt, n_heads, head_dim))
        cs_l.append(jnp.concatenate([state_conv[l][:, nt:], to_seq(us).astype(state_conv.dtype)], axis=1))

    y_prompt = h_p[:, N_META:n_real]
    y_sample = h_s.reshape(nt, nb, d_model).transpose(1, 0, 2)
    return (y_prompt, y_sample, jnp.stack(kp_l), jnp.stack(vp_l), jnp.stack(cp_l),
            jnp.stack(ks_l), jnp.stack(vs_l), jnp.stack(cs_l))
```
